```python
import math
import jax, jax.numpy as jnp
from jax import lax
import numpy as np

D_MODEL = 1024
BATCH = 2
SEQ = 8192
DEPTH = 2
DEC_BATCH = 32
DEC_SEQ = 8
PAST_LEN = 8192
PAGE_SIZE = 128

HEAD_DIM = 64
ATTN_WIDTH = D_MODEL // 2
N_HEADS_A = ATTN_WIDTH // HEAD_DIM
SB_BIAS_INIT = -8.0
SSM_WIDTH = D_MODEL // 4
SSM_GROUP = 16
SSM_GROUPS = SSM_WIDTH // SSM_GROUP
SSM_STATE = 64
POOL_WIDTH = D_MODEL // 4
POOL_WINDOWS = (2, 4, 8, 16)
POOL_GROUPS = len(POOL_WINDOWS)
POOL_GROUP_DIM = POOL_WIDTH // POOL_GROUPS
POOL_BUF = max(POOL_WINDOWS) - 1
MIX_WIDTH = ATTN_WIDTH + SSM_WIDTH + POOL_WIDTH
IN_WIDTH = 3 * ATTN_WIDTH + SSM_WIDTH + POOL_WIDTH
N_EXPERT_GROUPS = 4
EXPERTS_PER_GROUP = 8
N_EXPERTS = N_EXPERT_GROUPS * EXPERTS_PER_GROUP
TOP_K_INNER = 2
D_EXPERT = 256
Q_BLOCK = 128
EPS = 1e-6

kernel_name = 'hymba_stickbreak_s5_pool_hmoe_step'


def rmsnorm(x, g):
    x32 = x.astype(jnp.float32)
    y = x32 * lax.rsqrt(jnp.mean(x32 * x32, axis=-1, keepdims=True) + EPS)
    return (y * g.astype(jnp.float32)).astype(x.dtype)


def stick_breaking(q, k, v, q_pos, k_pos, sb_bias):
    z = jnp.einsum('bqhd,bkhd->bhqk', q.astype(jnp.float32), k.astype(jnp.float32)) * (HEAD_DIM ** -0.5)
    z = z + sb_bias.astype(jnp.float32)[None, :, None, None]
    valid = k_pos[None, :] < q_pos[:, None]
    log_keep = jnp.where(valid, jax.nn.log_sigmoid(-z), 0.0)
    between = lax.cumsum(log_keep, axis=3, reverse=True) - log_keep
    a = jnp.where(valid, jnp.exp(jax.nn.log_sigmoid(z) + between), 0.0)
    return jnp.einsum('bhqk,bkhd->bqhd', a, v.astype(jnp.float32))


def attn_prompt(q, k, v, sb_bias):
    b, s = q.shape[0], q.shape[1]
    nb = s // Q_BLOCK
    pos = jnp.arange(s, dtype=jnp.int32)
    qb = q.reshape(b, nb, Q_BLOCK, N_HEADS_A, HEAD_DIM).transpose(1, 0, 2, 3, 4)
    pb = pos.reshape(nb, Q_BLOCK)
    out = lax.map(lambda args: stick_breaking(args[0], k, v, args[1], pos, sb_bias), (qb, pb))
    return out.transpose(1, 0, 2, 3, 4).reshape(b, s, N_HEADS_A, HEAD_DIM)


def _complex_affine_combine(left, right):
    a1r, a1i, b1r, b1i = left
    a2r, a2i, b2r, b2i = right
    return (a2r * a1r - a2i * a1i,
            a2r * a1i + a2i * a1r,
            a2r * b1r - a2i * b1i + b2r,
            a2r * b1i + a2i * b1r + b2i)


def s5_mixer(u, h0_re, h0_im, lam_re, lam_im, log_dt, b_re, b_im, c_re, c_im, d_skip, glu_w, glu_b):
    f32 = jnp.float32
    bsz, L = u.shape[0], u.shape[1]
    u32 = u.astype(f32).reshape(bsz, L, SSM_GROUPS, SSM_GROUP)
    dt = jnp.exp(log_dt.astype(f32))[:, None]
    lr, li = lam_re.astype(f32), lam_im.astype(f32)
    mag = jnp.exp(lr * dt)
    ar, ai = mag * jnp.cos(li * dt), mag * jnp.sin(li * dt)
    er = ar - 1.0
    den = lr * lr + li * li
    fr = (er * lr + ai * li) / den
    fi = (ai * lr - er * li) / den
    br32, bi32 = b_re.astype(f32), b_im.astype(f32)
    bbar_re = fr[..., None] * br32 - fi[..., None] * bi32
    bbar_im = fr[..., None] * bi32 + fi[..., None] * br32
    bu_re = jnp.einsum('blgc,gnc->blgn', u32, bbar_re)
    bu_im = jnp.einsum('blgc,gnc->blgn', u32, bbar_im)
    h0r, h0i = h0_re.astype(f32), h0_im.astype(f32)
    bu_re = bu_re.at[:, 0].add(ar * h0r - ai * h0i)
    bu_im = bu_im.at[:, 0].add(ar * h0i + ai * h0r)
    a_re = jnp.broadcast_to(ar, bu_re.shape)
    a_im = jnp.broadcast_to(ai, bu_im.shape)
    _, _, h_re, h_im = lax.associative_scan(_complex_affine_combine, (a_re, a_im, bu_re, bu_im), axis=1)
    y = (jnp.einsum('blgn,gcn->blgc', h_re, c_re.astype(f32))
         - jnp.einsum('blgn,gcn->blgc', h_im, c_im.astype(f32)))
    y = (y + d_skip.astype(f32).reshape(SSM_GROUPS, SSM_GROUP) * u32).reshape(bsz, L, SSM_WIDTH)
    g = jax.nn.gelu(y)
    out = g * jax.nn.sigmoid(g @ glu_w.astype(f32) + glu_b.astype(f32))
    return out.astype(u.dtype), h_re[:, -1].astype(h0_re.dtype), h_im[:, -1].astype(h0_im.dtype)


def pool_mixer(u, buf, start_pos, pool_w, pool_scale):
    f32 = jnp.float32
    bsz, L = u.shape[0], u.shape[1]
    u32 = u.astype(f32)
    ext = jnp.concatenate([buf.astype(f32), u32], axis=1)
    cs = jnp.concatenate([jnp.zeros((bsz, 1, POOL_WIDTH), f32), jnp.cumsum(ext, axis=1)], axis=1)
    pos = start_pos + jnp.arange(L, dtype=jnp.int32)
    hi = cs[:, POOL_BUF + 1:POOL_BUF + 1 + L]
    means = []
    for gi, w in enumerate(POOL_WINDOWS):
        c0, c1 = gi * POOL_GROUP_DIM, (gi + 1) * POOL_GROUP_DIM
        lo = cs[:, POOL_BUF + 1 - w:POOL_BUF + 1 - w + L, c0:c1]
        cnt = jnp.minimum(pos + 1, w).astype(f32)[None, :, None]
        means.append((hi[..., c0:c1] - lo) / cnt)
    d = (jnp.concatenate(means, axis=-1) - u32).reshape(bsz, L, POOL_GROUPS, POOL_GROUP_DIM)
    mixed = jnp.einsum('blgc,gcd->blgd', d, pool_w.astype(f32)).reshape(bsz, L, POOL_WIDTH)
    out = mixed * pool_scale.astype(f32)
    return out.astype(u.dtype), ext[:, -POOL_BUF:].astype(buf.dtype)


def hier_moe(x, rg_w, rg_b, re_w, re_b, w_gate, w_up, w_down):
    f32 = jnp.float32
    bsz, L, d = x.shape
    t = x.reshape(-1, d)
    t32 = t.astype(f32)
    n_tok = t.shape[0]
    p_group = jax.nn.softmax(t32 @ rg_w.astype(f32) + rg_b.astype(f32), axis=-1)
    g_idx = jnp.argmax(p_group, axis=-1)
    g_w = jnp.max(p_group, axis=-1)
    e_logits = (t32 @ re_w.astype(f32) + re_b.astype(f32)).reshape(n_tok, N_EXPERT_GROUPS, EXPERTS_PER_GROUP)
    e_in = e_logits[jnp.arange(n_tok), g_idx]
    top_v, top_i = lax.top_k(e_in, TOP_K_INNER)
    w_sel = jax.nn.softmax(top_v, axis=-1) * g_w[:, None]
    e_id = g_idx[:, None] * EXPERTS_PER_GROUP + top_i
    gates = jnp.sum(jax.nn.one_hot(e_id, N_EXPERTS, dtype=f32) * w_sel[..., None], axis=1)
    h = jax.nn.silu(jnp.einsum('td,edh->teh', t, w_gate)) * jnp.einsum('td,edh->teh', t, w_up)
    y = jnp.einsum('teh,ehd->td', h * gates[..., None].astype(h.dtype), w_down)
    return y.reshape(bsz, L, d).astype(x.dtype)


def trunk_layer(x, start_pos, k_past, v_past, h0_re, h0_im, pool_buf, lw):
    (g_mix, w_in, q_g, k_g, sb_bias, lam_re, lam_im, log_dt, b_re, b_im, c_re, c_im, d_skip, glu_w, glu_b,
     pool_w, pool_scale, w_out, g_ffn, rg_w, rg_b, re_w, re_b, w_gate, w_up, w_down) = lw
    bsz, L = x.shape[0], x.shape[1]
    h = rmsnorm(x, g_mix)
    proj = h @ w_in
    q, k, v, u_ssm, u_pool = jnp.split(
        proj, [ATTN_WIDTH, 2 * ATTN_WIDTH, 3 * ATTN_WIDTH, 3 * ATTN_WIDTH + SSM_WIDTH], axis=-1)
    q = rmsnorm(q.reshape(bsz, L, N_HEADS_A, HEAD_DIM), q_g)
    k = rmsnorm(k.reshape(bsz, L, N_HEADS_A, HEAD_DIM), k_g)
    v = v.reshape(bsz, L, N_HEADS_A, HEAD_DIM)
    if k_past is None:
        attn = attn_prompt(q, k, v, sb_bias)
    else:
        k_all = jnp.concatenate([k_past.astype(k.dtype), k], axis=1)
        v_all = jnp.concatenate([v_past.astype(v.dtype), v], axis=1)
        k_pos = jnp.arange(k_all.shape[1], dtype=jnp.int32)
        q_pos = start_pos + jnp.arange(L, dtype=jnp.int32)
        attn = stick_breaking(q, k_all, v_all, q_pos, k_pos, sb_bias)
    attn = attn.reshape(bsz, L, ATTN_WIDTH).astype(x.dtype)
    ssm_out, h_re, h_im = s5_mixer(u_ssm, h0_re, h0_im, lam_re, lam_im, log_dt, b_re, b_im,
                                   c_re, c_im, d_skip, glu_w, glu_b)
    pool_out, new_buf = pool_mixer(u_pool, pool_buf, start_pos, pool_w, pool_scale)
    mix = jnp.concatenate([attn, ssm_out, pool_out], axis=-1)
    x = x + mix @ w_out
    x = x + hier_moe(rmsnorm(x, g_ffn), rg_w, rg_b, re_w, re_b, w_gate, w_up, w_down)
    return x, k, v, h_re, h_im, new_buf


def setup_inputs(seed: int = 0) -> dict:
    key = jax.random.key(seed)
    ks = jax.random.split(key, 40)
    f32 = jnp.float32

    def nrm(k, shape, scale):
        return scale * jax.random.normal(k, shape, f32)

    n_pages = PAST_LEN // PAGE_SIZE
    n_used = DEC_BATCH * n_pages
    n_phys = n_used + -(-n_used // 4)
    page_table = jax.random.permutation(ks[0], n_phys)[:n_used].reshape(DEC_BATCH, n_pages).astype(jnp.int32)
    kv_shape = (DEPTH, n_phys, PAGE_SIZE, N_HEADS_A, HEAD_DIM)
    ssm_state_shape = (DEPTH, DEC_BATCH, SSM_GROUPS, SSM_STATE)
    lam_shape = (DEPTH, SSM_GROUPS, SSM_STATE)
    return {
        'x_prompt': nrm(ks[1], (BATCH, SEQ, D_MODEL), 1.0),
        'x_sample': nrm(ks[2], (DEC_BATCH, DEC_SEQ, D_MODEL), 1.0),
        'cache_k': nrm(ks[3], kv_shape, 1.0),
        'cache_v': nrm(ks[4], kv_shape, 1.0),
        'state_ssm_re': nrm(ks[5], ssm_state_shape, 0.1),
        'state_ssm_im': nrm(ks[6], ssm_state_shape, 0.1),
        'state_pool': nrm(ks[7], (DEPTH, DEC_BATCH, POOL_BUF, POOL_WIDTH), 1.0),
        'page_table': page_table,
        'norm_mix_g': 1.0 + nrm(ks[8], (DEPTH, D_MODEL), 0.02),
        'w_in': nrm(ks[9], (DEPTH, D_MODEL, IN_WIDTH), D_MODEL ** -0.5),
        'q_norm_g': 1.0 + nrm(ks[10], (DEPTH, HEAD_DIM), 0.02),
        'k_norm_g': 1.0 + nrm(ks[11], (DEPTH, HEAD_DIM), 0.02),
        'sb_bias': SB_BIAS_INIT + nrm(ks[33], (DEPTH, N_HEADS_A), 0.1),
        'ssm_lambda_re': -0.5 + nrm(ks[12], lam_shape, 0.01),
        'ssm_lambda_im': math.pi * jnp.arange(SSM_STATE, dtype=f32) + nrm(ks[13], lam_shape, 0.01),
        'ssm_log_dt': jax.random.uniform(ks[14], (DEPTH, SSM_GROUPS), f32, math.log(1e-3), math.log(1e-1)),
        'ssm_b_re': nrm(ks[15], (DEPTH, SSM_GROUPS, SSM_STATE, SSM_GROUP), (2 * SSM_GROUP) ** -0.5),
        'ssm_b_im': nrm(ks[16], (DEPTH, SSM_GROUPS, SSM_STATE, SSM_GROUP), (2 * SSM_GROUP) ** -0.5),
        'ssm_c_re': nrm(ks[17], (DEPTH, SSM_GROUPS, SSM_GROUP, SSM_STATE), (2 * SSM_STATE) ** -0.5),
        'ssm_c_im': nrm(ks[18], (DEPTH, SSM_GROUPS, SSM_GROUP, SSM_STATE), (2 * SSM_STATE) ** -0.5),
        'ssm_d': nrm(ks[19], (DEPTH, SSM_WIDTH), 1.0),
        'ssm_glu_w': nrm(ks[20], (DEPTH, SSM_WIDTH, SSM_WIDTH), SSM_WIDTH ** -0.5),
        'ssm_glu_b': nrm(ks[21], (DEPTH, SSM_WIDTH), 0.01),
        'pool_w': nrm(ks[22], (DEPTH, POOL_GROUPS, POOL_GROUP_DIM, POOL_GROUP_DIM), POOL_GROUP_DIM ** -0.5),
        'pool_scale': 1.0 + nrm(ks[23], (DEPTH, POOL_WIDTH), 0.02),
        'w_out': nrm(ks[24], (DEPTH, MIX_WIDTH, D_MODEL), MIX_WIDTH ** -0.5),
        'norm_ffn_g': 1.0 + nrm(ks[25], (DEPTH, D_MODEL), 0.02),
        'router_group_w': nrm(ks[26], (DEPTH, D_MODEL, N_EXPERT_GROUPS), D_MODEL ** -0.5),
        'router_group_b': nrm(ks[27], (DEPTH, N_EXPERT_GROUPS), 0.01),
        'router_expert_w': nrm(ks[28], (DEPTH, D_MODEL, N_EXPERTS), D_MODEL ** -0.5),
        'router_expert_b': nrm(ks[29], (DEPTH, N_EXPERTS), 0.01),
        'expert_w_gate': nrm(ks[30], (DEPTH, N_EXPERTS, D_MODEL, D_EXPERT), D_MODEL ** -0.5),
        'expert_w_up': nrm(ks[31], (DEPTH, N_EXPERTS, D_MODEL, D_EXPERT), D_MODEL ** -0.5),
        'expert_w_down': nrm(ks[32], (DEPTH, N_EXPERTS, D_EXPERT, D_MODEL), D_EXPERT ** -0.5),
    }


def reference(x_prompt, x_sample, cache_k, cache_v, state_ssm_re, state_ssm_im, state_pool, page_table,
              norm_mix_g, w_in, q_norm_g, k_norm_g, sb_bias, ssm_lambda_re, ssm_lambda_im, ssm_log_dt,
              ssm_b_re, ssm_b_im, ssm_c_re, ssm_c_im, ssm_d, ssm_glu_w, ssm_glu_b, pool_w, pool_scale,
              w_out, norm_ffn_g, router_group_w, router_group_b, router_expert_w, router_expert_b,
              expert_w_gate, expert_w_up, expert_w_down):
    layer_params = (norm_mix_g, w_in, q_norm_g, k_norm_g, sb_bias, ssm_lambda_re, ssm_lambda_im, ssm_log_dt,
                    ssm_b_re, ssm_b_im, ssm_c_re, ssm_c_im, ssm_d, ssm_glu_w, ssm_glu_b, pool_w, pool_scale,
                    w_out, norm_ffn_g, router_group_w, router_group_b, router_expert_w, router_expert_b,
                    expert_w_gate, expert_w_up, expert_w_down)
    bp, sp = x_prompt.shape[0], x_prompt.shape[1]
    db = x_sample.shape[0]
    past_len = page_table.shape[1] * cache_k.shape[2]
    yp, ys = x_prompt, x_sample
    kp_l, vp_l, hrp_l, hip_l, pbp_l = [], [], [], [], []
    ks_l, vs_l, hrs_l, his_l, pbs_l = [], [], [], [], []
    for l in range(DEPTH):
        lw = tuple(p[l] for p in layer_params)
        zero_h = jnp.zeros((bp, SSM_GROUPS, SSM_STATE), state_ssm_re.dtype)
        zero_buf = jnp.zeros((bp, POOL_BUF, POOL_WIDTH), state_pool.dtype)
        yp, kp, vp, hrp, hip, pbp = trunk_layer(yp, 0, None, None, zero_h, zero_h, zero_buf, lw)
        k_past = cache_k[l][page_table].reshape(db, past_len, N_HEADS_A, HEAD_DIM)
        v_past = cache_v[l][page_table].reshape(db, past_len, N_HEADS_A, HEAD_DIM)
        ys, ksm, vsm, hrs, his, pbs = trunk_layer(ys, past_len, k_past, v_past, state_ssm_re[l],
                                                  state_ssm_im[l], state_pool[l], lw)
        kp_l.append(kp.reshape(bp * (sp // PAGE_SIZE), PAGE_SIZE, N_HEADS_A, HEAD_DIM))
        vp_l.append(vp.reshape(bp * (sp // PAGE_SIZE), PAGE_SIZE, N_HEADS_A, HEAD_DIM))
        hrp_l.append(hrp)
        hip_l.append(hip)
        pbp_l.append(pbp)
        ks_l.append(ksm)
        vs_l.append(vsm)
        hrs_l.append(hrs)
        his_l.append(his)
        pbs_l.append(pbs)
    new_k_prompt = jnp.stack(kp_l)
    new_v_prompt = jnp.stack(vp_l)
    new_ssm_re_prompt = jnp.stack(hrp_l)
    new_ssm_im_prompt = jnp.stack(hip_l)
    new_pool_prompt = jnp.stack(pbp_l)
    new_k_sample = jnp.stack(ks_l)
    new_v_sample = jnp.stack(vs_l)
    new_ssm_re_sample = jnp.stack(hrs_l)
    new_ssm_im_sample = jnp.stack(his_l)
    new_pool_sample = jnp.stack(pbs_l)
    return (yp, ys, new_k_prompt, new_v_prompt, new_ssm_re_prompt, new_ssm_im_prompt, new_pool_prompt,
            new_k_sample, new_v_sample, new_ssm_re_sample, new_ssm_im_sample, new_pool_sample)
```

```python
import functools
import math

import numpy as np
import jax
import jax.numpy as jnp
from jax import lax
from jax.experimental import pallas as pl
from jax.experimental.pallas import tpu as pltpu

F32 = jnp.float32
BF16 = jnp.bfloat16
I32 = jnp.int32

EPS = 1e-6
HEAD_DIM = 64
POOL_WINDOWS = (2, 4, 8, 16)
POOL_HIST = 16
TOP_K_INNER = 2
LANES = 128
VMEM_LIMIT = 56 * 1024 * 1024

NEG_INF = float("-inf")


def _cparams(*sem):
    return pltpu.CompilerParams(dimension_semantics=sem, vmem_limit_bytes=VMEM_LIMIT)


def _split_bf16(x):
    hi = x.astype(BF16)
    lo = (x - hi.astype(F32)).astype(BF16)
    return hi, lo


def _dot(a, b):
    return jnp.dot(a, b, preferred_element_type=F32)


def _dot_nt(a, b):
    return lax.dot_general(a, b, (((1,), (1,)), ((), ())), preferred_element_type=F32)


def _dot_tn(a, b):
    return lax.dot_general(a, b, (((0,), (0,)), ((), ())), preferred_element_type=F32)


def _neg_softplus(z):
    return -(jnp.maximum(z, 0.0) + jnp.log(1.0 + jnp.exp(-jnp.abs(z))))


def _inproj_kernel(x_ref, g_ref, w_ref, qg_ref, kg_ref, ones_ref,
                   q_ref, k_ref, v_ref, kb_ref, vb_ref, us_ref, up_ref, *, aw, sw):
    x = x_ref[...]
    ms = jnp.mean(x * x, axis=-1, keepdims=True)
    h = (x * lax.rsqrt(ms + EPS) * g_ref[...]).astype(BF16)
    proj = _dot(h, w_ref[...])
    ones_bd = ones_ref[...]

    def headnorm(t, gain):
        hi, lo = _split_bf16(t * t)
        ss = _dot(hi, ones_bd) + _dot(lo, ones_bd)
        return t * lax.rsqrt(ss * (1.0 / HEAD_DIM) + EPS) * gain

    q = headnorm(proj[:, :aw], qg_ref[...])
    k = headnorm(proj[:, aw:2 * aw], kg_ref[...])
    v = proj[:, 2 * aw:3 * aw]
    q_ref[...] = (q * (HEAD_DIM ** -0.5)).astype(BF16)
    k_ref[...] = k
    v_ref[...] = v
    kb_ref[...] = k.astype(BF16)
    vb_ref[...] = v.astype(BF16)
    us_ref[...] = proj[:, 3 * aw:3 * aw + sw]
    up_ref[...] = proj[:, 3 * aw + sw:]


def _inproj(x2, g_mix, w_in_bf, qg_t, kg_t, ones_bd, tm):
    n, d = x2.shape
    aw = ones_bd.shape[0]
    sw = (w_in_bf.shape[1] - 3 * aw) // 2
    row = lambda i: (i, 0)
    const = lambda i: (0, 0)
    sds = jax.ShapeDtypeStruct
    return pl.pallas_call(
        functools.partial(_inproj_kernel, aw=aw, sw=sw),
        out_shape=[sds((n, aw), BF16), sds((n, aw), F32), sds((n, aw), F32),
                   sds((n, aw), BF16), sds((n, aw), BF16), sds((n, sw), F32), sds((n, sw), F32)],
        grid=(n // tm,),
        in_specs=[pl.BlockSpec((tm, d), row), pl.BlockSpec((1, d), const),
                  pl.BlockSpec(w_in_bf.shape, const), pl.BlockSpec((1, aw), const),
                  pl.BlockSpec((1, aw), const), pl.BlockSpec((aw, aw), const)],
        out_specs=[pl.BlockSpec((tm, aw), row)] * 5 + [pl.BlockSpec((tm, sw), row)] * 2,
        compiler_params=_cparams("parallel"),
        name="inproj",
    )(x2, g_mix, w_in_bf, qg_t, kg_t, ones_bd)


def _attn_prompt_kernel(qi_ref, kj_ref, q_ref, k_ref, v_ref, uu_ref, bias_ref, o_ref,
                        r_ref, acc_ref, *, tq, tkb, sub, rc):
    s = pl.program_id(2)
    hp = pl.program_id(1)
    qi = qi_ref[s]
    kj = kj_ref[s]
    is_diag = kj == qi

    @pl.when(is_diag)
    def _():
        r_ref[...] = jnp.zeros_like(r_ref)
        acc_ref[...] = jnp.zeros_like(acc_ref)

    n_sub = tkb // sub
    n_rc = tq // rc
    lane_q = lax.broadcasted_iota(I32, (rc, LANES), 1)
    lane_k = lax.broadcasted_iota(I32, (sub, LANES), 1)
    d_idx = lax.broadcasted_iota(I32, (rc, sub), 1) - lax.broadcasted_iota(I32, (rc, sub), 0)
    biases = (bias_ref[pl.ds(2 * hp, 1), :], bias_ref[pl.ds(2 * hp + 1, 1), :])
    uu = uu_ref[...]
    zero_bf = jnp.zeros((), BF16)

    def run(masked):
        def c_body(ci, carry):
            c = n_sub - 1 - ci
            koff = pl.multiple_of(c * sub, sub)
            kc = k_ref[pl.ds(koff, sub), :]
            vc = v_ref[pl.ds(koff, sub), :]
            v_heads = (jnp.where(lane_k < HEAD_DIM, vc, zero_bf), jnp.where(lane_k >= HEAD_DIM, vc, zero_bf))
            r0 = (c * sub) // rc if masked else 0

            def r_body(r, carry2):
                roff = pl.multiple_of(r * rc, rc)
                q2 = q_ref[pl.ds(roff, rc), :]
                q_heads = (jnp.where(lane_q < HEAD_DIM, q2, zero_bf), jnp.where(lane_q >= HEAD_DIM, q2, zero_bf))
                if masked:
                    valid = d_idx < (roff - koff)
                out = jnp.zeros((rc, LANES), F32)
                for hd in range(2):
                    z = _dot_nt(q_heads[hd], kc) + biases[hd]
                    lk = _neg_softplus(z)
                    if masked:
                        lk = jnp.where(valid, lk, 0.0)
                    hi, lo = _split_bf16(lk)
                    ct = _dot(jnp.concatenate([hi, lo], axis=1), uu)
                    rr = r_ref[hd, pl.ds(roff, rc), :]
                    p = jnp.exp(z + lk + ct[:, :sub] + rr)
                    if masked:
                        p = jnp.where(valid, p, 0.0)
                    r_ref[hd, pl.ds(roff, rc), :] = rr + ct[:, sub:]
                    out = out + _dot(p.astype(BF16), v_heads[hd])
                acc_ref[pl.ds(roff, rc), :] += out
                return carry2

            lax.fori_loop(r0, n_rc, r_body, 0)
            return carry

        lax.fori_loop(0, n_sub, c_body, 0)

    @pl.when(is_diag)
    def _():
        run(True)

    @pl.when(jnp.logical_not(is_diag))
    def _():
        run(False)

    @pl.when(kj == 0)
    def _():
        o_ref[...] = acc_ref[...].astype(o_ref.dtype)


def _suffix_matrix(sub):
    j = np.arange(2 * sub)[:, None] % sub
    s = np.arange(2 * sub)[None, :]
    m = np.where(s < sub, j > s, True)
    return jnp.asarray(m, dtype=BF16)


def _attn_prompt(q_bf, k_bf, v_bf, bias_rows, bsz, seq, tq):
    n, aw = q_bf.shape
    tkb = tq
    sub = LANES
    rc = 256
    nq = seq // tq
    qi_tbl = np.concatenate([np.full(i + 1, i) for i in range(nq)]).astype(np.int32)
    kj_tbl = np.concatenate([np.arange(i, -1, -1) for i in range(nq)]).astype(np.int32)
    n_steps = qi_tbl.shape[0]
    n_hp = aw // LANES
    uu = _suffix_matrix(sub)
    grid_spec = pltpu.PrefetchScalarGridSpec(
        num_scalar_prefetch=2,
        grid=(bsz, n_hp, n_steps),
        in_specs=[
            pl.BlockSpec((tq, LANES), lambda b, h, s, qi, kj: (b * nq + qi[s], h)),
            pl.BlockSpec((tkb, LANES), lambda b, h, s, qi, kj: (b * nq + kj[s], h)),
            pl.BlockSpec((tkb, LANES), lambda b, h, s, qi, kj: (b * nq + kj[s], h)),
            pl.BlockSpec(uu.shape, lambda b, h, s, qi, kj: (0, 0)),
            pl.BlockSpec(bias_rows.shape, lambda b, h, s, qi, kj: (0, 0)),
        ],
        out_specs=pl.BlockSpec((tq, LANES), lambda b, h, s, qi, kj: (b * nq + qi[s], h)),
        scratch_shapes=[pltpu.VMEM((2, tq, LANES), F32), pltpu.VMEM((tq, LANES), F32)],
    )
    return pl.pallas_call(
        functools.partial(_attn_prompt_kernel, tq=tq, tkb=tkb, sub=sub, rc=rc),
        out_shape=jax.ShapeDtypeStruct((n, aw), BF16),
        grid_spec=grid_spec,
        compiler_params=_cparams("parallel", "parallel", "arbitrary"),
        name="attn_prompt",
    )(jnp.asarray(qi_tbl), jnp.asarray(kj_tbl), q_bf, k_bf, v_bf, uu, bias_rows)


def _attn_sample_kernel(pt_ref, q_ref, kn_ref, vn_ref, bias_ref, uu_ref, *rest, n_pages_step, page, nq, nh):
    k_refs = rest[:n_pages_step]
    v_refs = rest[n_pages_step:2 * n_pages_step]
    o_ref = rest[2 * n_pages_step]
    qbd_ref, r_ref, acc_ref = rest[2 * n_pages_step + 1:]
    del pt_ref
    j = pl.program_id(1)
    ncol = nh * nq
    aw = nh * HEAD_DIM
    col_head = lax.broadcasted_iota(I32, (ncol, aw), 0) // nq
    lane_head = lax.broadcasted_iota(I32, (ncol, aw), 1) // HEAD_DIM
    same_head = col_head == lane_head
    bias = bias_ref[...]

    @pl.when(j == 0)
    def _():
        q = q_ref[...].astype(F32)
        qt = jnp.concatenate([q] * nh, axis=0)
        qbd = jnp.where(same_head, qt, 0.0).astype(BF16)
        qbd_ref[...] = qbd
        pad = jnp.zeros((nq, aw), F32)
        kn = jnp.concatenate([kn_ref[...], pad], axis=0).astype(BF16)
        vn = jnp.concatenate([vn_ref[...], pad], axis=0).astype(BF16)
        z = _dot_nt(kn, qbd) + bias
        krow = lax.broadcasted_iota(I32, (2 * nq, ncol), 0)
        qcol = lax.broadcasted_iota(I32, (2 * nq, ncol), 1) % nq
        valid = krow < qcol
        lk = jnp.where(valid, _neg_softplus(z), 0.0)
        between = jnp.zeros_like(lk)
        for jn in range(1, nq):
            between = between + jnp.where(krow < jn, lk[jn:jn + 1, :], 0.0)
        p = jnp.where(valid, jnp.exp(z + lk + between), 0.0)
        r_ref[...] = jnp.sum(lk, axis=0, keepdims=True)
        acc_ref[...] = _dot_tn(p.astype(BF16), vn)

    kk = jnp.concatenate([r[...].astype(BF16) for r in k_refs], axis=0)
    vv = jnp.concatenate([r[...].astype(BF16) for r in v_refs], axis=0)
    z = _dot_nt(kk, qbd_ref[...]) + bias
    lk = _neg_softplus(z)
    hi, lo = _split_bf16(lk)
    uu = uu_ref[...]
    between = _dot(uu, hi) + _dot(uu, lo)
    r_old = r_ref[...]
    p = jnp.exp(z + lk + between + r_old)
    r_ref[...] = r_old + jnp.sum(lk, axis=0, keepdims=True)
    acc_ref[...] += _dot_tn(p.astype(BF16), vv)

    @pl.when(j == pl.num_programs(1) - 1)
    def _():
        acc = jnp.where(same_head, acc_ref[...], 0.0)
        out = acc[0:nq, :]
        for h in range(1, nh):
            out = out + acc[h * nq:(h + 1) * nq, :]
        o_ref[...] = out.astype(o_ref.dtype)


def _attn_sample(q3, kn3, vn3, cache_k4, cache_v4, page_table, bias_cols, layer, n_pages_step):
    db, nq, aw = q3.shape
    nh = aw // HEAD_DIM
    page = cache_k4.shape[2]
    n_pages = page_table.shape[1]
    n_steps = n_pages // n_pages_step
    blk = n_pages_step * page
    uu = jnp.asarray(np.triu(np.ones((blk, blk), np.float32), 1), dtype=BF16)
    ncol = nh * nq

    def page_map(i):
        return lambda b, j, pt: (layer, pt[b, (n_steps - 1 - j) * n_pages_step + i], 0, 0)

    seq_map = lambda b, j, pt: (b, 0, 0)
    const = lambda b, j, pt: (0, 0)
    page_spec = [pl.BlockSpec((None, None, page, aw), page_map(i)) for i in range(n_pages_step)]
    grid_spec = pltpu.PrefetchScalarGridSpec(
        num_scalar_prefetch=1,
        grid=(db, n_steps),
        in_specs=[pl.BlockSpec((None, nq, aw), seq_map)] * 3
        + [pl.BlockSpec((1, ncol), const), pl.BlockSpec((blk, blk), const)] + page_spec + page_spec,
        out_specs=pl.BlockSpec((None, nq, aw), seq_map),
        scratch_shapes=[pltpu.VMEM((ncol, aw), BF16), pltpu.VMEM((1, ncol), F32), pltpu.VMEM((ncol, aw), F32)],
    )
    return pl.pallas_call(
        functools.partial(_attn_sample_kernel, n_pages_step=n_pages_step, page=page, nq=nq, nh=nh),
        out_shape=jax.ShapeDtypeStruct((db, nq, aw), BF16),
        grid_spec=grid_spec,
        compiler_params=_cparams("parallel", "arbitrary"),
        name="attn_sample",
    )(page_table, q3, kn3, vn3, bias_cols, uu, *([cache_k4] * n_pages_step), *([cache_v4] * n_pages_step))


def _ssm_prep_kernel(lr_ref, li_ref, ldt_ref, bre_ref, bim_ref, cre_ref, cim_ref,
                     a_ref, bbar_ref, cbd_ref, pre_ref, pim_ref, qre_ref, qim_ref, *, t_rows):
    lr = lr_ref[...]
    li = li_ref[...]
    dt = jnp.exp(ldt_ref[...])
    mag = jnp.exp(lr * dt)
    ar = mag * jnp.cos(li * dt)
    ai = mag * jnp.sin(li * dt)
    er = ar - 1.0
    den = lr * lr + li * li
    fr = (er * lr + ai * li) / den
    fi = (ai * lr - er * li) / den
    a_ref[0:1, :] = ar
    a_ref[1:2, :] = ai
    bre = bre_ref[...]
    bim = bim_ref[...]
    ns = lr.shape[1]
    bbar_ref[:, :ns] = (fr * bre - fi * bim).astype(BF16)
    bbar_ref[:, ns:] = (fr * bim + fi * bre).astype(BF16)
    cbd_ref[:ns, :] = cre_ref[...].astype(BF16)
    cbd_ref[ns:, :] = (-cim_ref[...]).astype(BF16)
    t = lax.broadcasted_iota(I32, (t_rows, ns), 0).astype(F32)
    grow = jnp.exp(t * (lr * dt))
    shrink = jnp.exp(-t * (lr * dt))
    ang = t * (li * dt)
    c = jnp.cos(ang)
    s = jnp.sin(ang)
    pre_ref[...] = grow * c
    pim_ref[...] = grow * s
    qre_ref[...] = shrink * c
    qim_ref[...] = -(shrink * s)


def _ssm_prep(lr, li, ldt, bre_bd, bim_bd, cre_bd, cim_bd, t_rows):
    ns = lr.shape[1]
    sw = bre_bd.shape[0]
    sds = jax.ShapeDtypeStruct
    return pl.pallas_call(
        functools.partial(_ssm_prep_kernel, t_rows=t_rows),
        out_shape=[sds((2, ns), F32), sds((sw, 2 * ns), BF16), sds((2 * ns, sw), BF16)]
        + [sds((t_rows, ns), F32)] * 4,
        compiler_params=pltpu.CompilerParams(vmem_limit_bytes=VMEM_LIMIT),
        name="ssm_prep",
    )(lr, li, ldt, bre_bd, bim_bd, cre_bd, cim_bd)


def _ssm_compute(u32, bbar, cbd, a, pre, pim, qre, qim, l2, hp_re, hp_im, dskip, gluw, glub):
    ns = pre.shape[1]
    bu = _dot(u32.astype(BF16), bbar)
    bur, bui = bu[:, :ns], bu[:, ns:]
    xr = qre * bur - qim * bui
    xi = qre * bui + qim * bur
    x_hi, x_lo = _split_bf16(jnp.concatenate([xr, xi], axis=1))
    s = _dot(l2, jnp.concatenate([x_hi, x_lo], axis=0))
    ar, ai = a[0:1, :], a[1:2, :]
    sr = s[:, :ns] + (ar * hp_re - ai * hp_im)
    si = s[:, ns:] + (ar * hp_im + ai * hp_re)
    hr = pre * sr - pim * si
    hi_ = pre * si + pim * sr
    hcat = jnp.concatenate([hr, hi_], axis=1)
    y = _dot(hcat.astype(BF16), cbd) + dskip * u32
    g = 0.5 * y * (1.0 + jnp.tanh(math.sqrt(2.0 / math.pi) * (y + 0.044715 * (y * y * y))))
    gate = 1.0 / (1.0 + jnp.exp(-(_dot(g.astype(BF16), gluw) + glub)))
    return g * gate, hcat


def _pool_compute(ext, pos, pw_bd, pscale):
    s2 = ext + pltpu.roll(ext, 1, 0)
    s4 = s2 + pltpu.roll(s2, 2, 0)
    s8 = s4 + pltpu.roll(s4, 4, 0)
    s16 = s8 + pltpu.roll(s8, 8, 0)
    grp = lax.broadcasted_iota(I32, ext.shape, 1) // (ext.shape[1] // len(POOL_WINDOWS))
    win = jnp.where(grp == 0, s2, jnp.where(grp == 1, s4, jnp.where(grp == 2, s8, s16)))
    w = jnp.where(grp == 0, POOL_WINDOWS[0],
                  jnp.where(grp == 1, POOL_WINDOWS[1], jnp.where(grp == 2, POOL_WINDOWS[2], POOL_WINDOWS[3])))
    cnt = jnp.maximum(jnp.minimum(pos + 1, w), 1).astype(F32)
    d = win / cnt - ext
    return _dot(d.astype(BF16), pw_bd) * pscale


def _mixers_prompt_kernel(us_ref, up_ref, bbar_ref, cbd_ref, a_ref, pre_ref, pim_ref, qre_ref, qim_ref, l2_ref,
                          dskip_ref, gluw_ref, glub_ref, pw_ref, ps_ref,
                          so_ref, po_ref, hl_ref, carry_ref, ext_ref, *, t_rows):
    c = pl.program_id(1)
    ns = pre_ref.shape[1]

    @pl.when(c == 0)
    def _():
        carry_ref[...] = jnp.zeros_like(carry_ref)
        ext_ref[0:POOL_HIST, :] = jnp.zeros((POOL_HIST, ext_ref.shape[1]), F32)

    carry = carry_ref[...]
    out, hcat = _ssm_compute(us_ref[...], bbar_ref[...], cbd_ref[...], a_ref[...], pre_ref[...], pim_ref[...],
                             qre_ref[...], qim_ref[...], l2_ref[...], carry[:, :ns], carry[:, ns:],
                             dskip_ref[...], gluw_ref[...], glub_ref[...])
    so_ref[...] = out.astype(so_ref.dtype)
    last = hcat[t_rows - 1:t_rows, :]
    carry_ref[...] = last
    hl_ref[...] = last

    ext_ref[POOL_HIST:, :] = up_ref[...]
    ext = ext_ref[...]
    pos = c * t_rows - POOL_HIST + lax.broadcasted_iota(I32, (t_rows + POOL_HIST, 1), 0)
    pooled = _pool_compute(ext, pos, pw_ref[...], ps_ref[...])
    po_ref[...] = pooled[POOL_HIST:, :].astype(po_ref.dtype)
    ext_ref[0:POOL_HIST, :] = ext[t_rows:, :]


def _mixers_prompt(us, up, prep, l2, dskip, gluw, glub, pw_bd, pscale, bsz, seq, t_rows):
    a, bbar, cbd, pre, pim, qre, qim = prep
    n, sw = us.shape
    ns = pre.shape[1]
    nc = seq // t_rows
    row = lambda b, c: (b * nc + c, 0)
    const = lambda b, c: (0, 0)
    full = lambda arr: pl.BlockSpec(arr.shape, const)
    sds = jax.ShapeDtypeStruct
    return pl.pallas_call(
        functools.partial(_mixers_prompt_kernel, t_rows=t_rows),
        out_shape=[sds((n, sw), BF16), sds((n, sw), BF16), sds((bsz, 1, 2 * ns), F32)],
        grid=(bsz, nc),
        in_specs=[pl.BlockSpec((t_rows, sw), row), pl.BlockSpec((t_rows, sw), row),
                  full(bbar), full(cbd), full(a), full(pre), full(pim), full(qre), full(qim), full(l2),
                  full(dskip), full(gluw), full(glub), full(pw_bd), full(pscale)],
        out_specs=[pl.BlockSpec((t_rows, sw), row), pl.BlockSpec((t_rows, sw), row),
                   pl.BlockSpec((None, 1, 2 * ns), lambda b, c: (b, 0, 0))],
        scratch_shapes=[pltpu.VMEM((1, 2 * ns), F32), pltpu.VMEM((t_rows + POOL_HIST, sw), F32)],
        compiler_params=_cparams("parallel", "arbitrary"),
        name="mixers_prompt",
    )(us, up, bbar, cbd, a, pre, pim, qre, qim, l2, dskip, gluw, glub, pw_bd, pscale)


def _mixers_sample_kernel(us_ref, ext_ref, h0_ref, bbar_ref, cbd_ref, a_ref, pre_ref, pim_ref, qre_ref, qim_ref,
                          l2_ref, dskip_ref, gluw_ref, glub_ref, pw_ref, ps_ref,
                          so_ref, po_ref, h_ref, *, seg_rows, start_pos):
    ns = pre_ref.shape[1]
    h0 = h0_ref[...]
    out, hcat = _ssm_compute(us_ref[...], bbar_ref[...], cbd_ref[...], a_ref[...], pre_ref[...], pim_ref[...],
                             qre_ref[...], qim_ref[...], l2_ref[...], h0[:, :ns], h0[:, ns:],
                             dskip_ref[...], gluw_ref[...], glub_ref[...])
    so_ref[...] = out.astype(so_ref.dtype)
    h_ref[...] = hcat
    ext = ext_ref[...]
    pos = start_pos - POOL_HIST + lax.broadcasted_iota(I32, (ext.shape[0], 1), 0) % seg_rows
    po_ref[...] = _pool_compute(ext, pos, pw_ref[...], ps_ref[...]).astype(po_ref.dtype)


def _mixers_sample(us, ext, h0rep, prep_s, l2, dskip, gluw, glub, pw_bd, pscale, seg_rows, start_pos):
    a, bbar, cbd, pre, pim, qre, qim = prep_s
    n, sw = us.shape
    ns = pre.shape[1]
    sds = jax.ShapeDtypeStruct
    return pl.pallas_call(
        functools.partial(_mixers_sample_kernel, seg_rows=seg_rows, start_pos=start_pos),
        out_shape=[sds((n, sw), BF16), sds(ext.shape, BF16), sds((n, 2 * ns), F32)],
        compiler_params=pltpu.CompilerParams(vmem_limit_bytes=VMEM_LIMIT),
        name="mixers_sample",
    )(us, ext, h0rep, bbar, cbd, a, pre, pim, qre, qim, l2, dskip, gluw, glub, pw_bd, pscale)


def _outproj_kernel(x_ref, at_ref, so_ref, po_ref, wo_ref, g_ref, rwh_ref, rwl_ref, rb_ref,
                    x1_ref, t_ref, gates_ref, *, n_exp, n_grp):
    mix = jnp.concatenate([at_ref[...], so_ref[...], po_ref[...]], axis=1)
    x1 = x_ref[...] + _dot(mix, wo_ref[...])
    x1_ref[...] = x1
    ms = jnp.mean(x1 * x1, axis=-1, keepdims=True)
    t = x1 * lax.rsqrt(ms + EPS) * g_ref[...]
    t_ref[...] = t.astype(BF16)
    t_hi, t_lo = _split_bf16(t)
    rwh = rwh_ref[...]
    logits = _dot(t_hi, rwh) + _dot(t_hi, rwl_ref[...]) + _dot(t_lo, rwh) + rb_ref[...]
    lane = lax.broadcasted_iota(I32, logits.shape, 1)
    big = jnp.int32(2 ** 30)
    epg = n_exp // n_grp
    is_g = (lane >= n_exp) & (lane < n_exp + n_grp)
    gl = jnp.where(is_g, logits, NEG_INF)
    gmax = jnp.max(gl, axis=1, keepdims=True)
    gidx = jnp.min(jnp.where(gl == gmax, lane - n_exp, big), axis=1, keepdims=True)
    g_w = 1.0 / jnp.sum(jnp.where(is_g, jnp.exp(gl - gmax), 0.0), axis=1, keepdims=True)
    in_grp = (lane < n_exp) & (lane // epg == gidx)
    el = jnp.where(in_grp, logits, NEG_INF)
    m1 = jnp.max(el, axis=1, keepdims=True)
    i1 = jnp.min(jnp.where(el == m1, lane, big), axis=1, keepdims=True)
    el2 = jnp.where(lane == i1, NEG_INF, el)
    m2 = jnp.max(el2, axis=1, keepdims=True)
    i2 = jnp.min(jnp.where(el2 == m2, lane, big), axis=1, keepdims=True)
    e21 = jnp.exp(m2 - m1)
    w1 = 1.0 / (1.0 + e21)
    w2 = e21 * w1
    gates_ref[...] = jnp.where(lane == i1, w1 * g_w, jnp.where(lane == i2, w2 * g_w, 0.0))


def _outproj(x2, attn, sso, poo, wo_bf, g_ffn, rw_hi, rw_lo, rb, n_exp, n_grp, tm):
    n, d = x2.shape
    aw, sw = attn.shape[1], sso.shape[1]
    row = lambda i: (i, 0)
    const = lambda i: (0, 0)
    sds = jax.ShapeDtypeStruct
    return pl.pallas_call(
        functools.partial(_outproj_kernel, n_exp=n_exp, n_grp=n_grp),
        out_shape=[sds((n, d), F32), sds((n, d), BF16), sds((n, LANES), F32)],
        grid=(n // tm,),
        in_specs=[pl.BlockSpec((tm, d), row), pl.BlockSpec((tm, aw), row), pl.BlockSpec((tm, sw), row),
                  pl.BlockSpec((tm, sw), row), pl.BlockSpec(wo_bf.shape, const), pl.BlockSpec((1, d), const),
                  pl.BlockSpec(rw_hi.shape, const), pl.BlockSpec(rw_lo.shape, const), pl.BlockSpec((1, LANES), const)],
        out_specs=[pl.BlockSpec((tm, d), row), pl.BlockSpec((tm, d), row), pl.BlockSpec((tm, LANES), row)],
        compiler_params=_cparams("parallel"),
        name="outproj",
    )(x2, attn, sso, poo, wo_bf, g_ffn, rw_hi, rw_lo, rb)


def _moe_kernel(t_ref, gates_ref, x1_ref, wg_ref, wu_ref, wd_ref, o_ref, *, epb):
    j = pl.program_id(1)

    @pl.when(j == 0)
    def _():
        o_ref[...] = x1_ref[...]

    t = t_ref[...]
    g_hi, g_lo = _split_bf16(gates_ref[...])
    g2 = jnp.concatenate([g_hi, g_lo], axis=1)
    dh = wg_ref.shape[2]
    sel_row = lax.broadcasted_iota(I32, (2 * LANES, dh), 0) % LANES
    acc = jnp.zeros(o_ref.shape, F32)
    for i in range(epb):
        e = j * epb + i
        gb = _dot(g2, (sel_row == e).astype(BF16))
        a = _dot(t, wg_ref[i])
        b = _dot(t, wu_ref[i])
        hh = (a / (1.0 + jnp.exp(-a))) * b * gb
        acc = acc + _dot(hh.astype(BF16), wd_ref[i])
    o_ref[...] += acc


def _moe(t_bf, gates, x1, wg, wu, wd, layer, tm, epb):
    n, d = x1.shape
    n_exp, _, dh = wg.shape[1:]
    row = lambda i, j: (i, 0)
    return pl.pallas_call(
        functools.partial(_moe_kernel, epb=epb),
        out_shape=jax.ShapeDtypeStruct((n, d), F32),
        grid=(n // tm, n_exp // epb),
        in_specs=[pl.BlockSpec((tm, d), row), pl.BlockSpec((tm, LANES), row), pl.BlockSpec((tm, d), row),
                  pl.BlockSpec((None, epb, d, dh), lambda i, j: (layer, j, 0, 0)),
                  pl.BlockSpec((None, epb, d, dh), lambda i, j: (layer, j, 0, 0)),
                  pl.BlockSpec((None, epb, dh, d), lambda i, j: (layer, j, 0, 0))],
        out_specs=pl.BlockSpec((tm, d), row),
        compiler_params=_cparams("parallel", "arbitrary"),
        name="moe",
    )(t_bf, gates, x1, wg, wu, wd)


def _block_diag_rows(w):
    g, a, b = w.shape
    eye = jnp.eye(g, dtype=w.dtype)
    return jnp.einsum('gab,gh->gahb', w, eye).reshape(g * a, g * b)


def _seg_cumsum_matrix(rows, seg):
    i = np.arange(rows)[:, None]
    j = np.arange(rows)[None, :]
    l = (j <= i) & (i // seg == j // seg)
    return jnp.asarray(np.concatenate([l, l], axis=1), dtype=BF16)


def kernel(x_prompt, x_sample, cache_k, cache_v, state_ssm_re, state_ssm_im, state_pool, page_table, norm_mix_g, w_in, q_norm_g, k_norm_g, sb_bias, ssm_lambda_re, ssm_lambda_im, ssm_log_dt, ssm_b_re, ssm_b_im, ssm_c_re, ssm_c_im, ssm_d, ssm_glu_w, ssm_glu_b, pool_w, pool_scale, w_out, norm_ffn_g, router_group_w, router_group_b, router_expert_w, router_expert_b, expert_w_gate, expert_w_up, expert_w_down):
    depth = w_in.shape[0]
    bsz, seq, d = x_prompt.shape
    db, dseq, _ = x_sample.shape
    n_phys, page, nh, hd = cache_k.shape[1:]
    assert hd == HEAD_DIM
    aw = nh * hd
    n_grp_ssm, n_state, grp_ch = ssm_b_re.shape[1:]
    sw = n_grp_ssm * grp_ch
    ns = n_grp_ssm * n_state
    pool_buf = state_pool.shape[2]
    assert pool_buf < POOL_HIST and max(POOL_WINDOWS) - 1 <= pool_buf
    n_exp_grp = router_group_w.shape[2]
    n_exp = router_expert_w.shape[2]
    past_len = page_table.shape[1] * page
    np_tok = bsz * seq
    ns_tok = db * dseq
    t_rows = 128

    w_in_bf = w_in.astype(BF16)
    w_out_bf = w_out.astype(BF16)
    wg_bf = expert_w_gate.astype(BF16)
    wu_bf = expert_w_up.astype(BF16)
    wd_bf = expert_w_down.astype(BF16)
    glu_w_bf = ssm_glu_w.astype(BF16)
    ones_bd = jnp.asarray(np.kron(np.eye(nh), np.ones((hd, hd))), dtype=BF16)
    cache_k4 = cache_k.reshape(depth, n_phys, page, aw)
    cache_v4 = cache_v.reshape(depth, n_phys, page, aw)
    l2_prompt = _seg_cumsum_matrix(t_rows, t_rows)
    l2_sample = _seg_cumsum_matrix(ns_tok, dseq)
    seg_rows = POOL_HIST + dseq

    xp = x_prompt.reshape(np_tok, d)
    xs = x_sample.reshape(ns_tok, d)
    outs = {k: [] for k in ("kp", "vp", "hrp", "hip", "pbp", "ks", "vs", "hrs", "his", "pbs")}

    for l in range(depth):
        g_mix = norm_mix_g[l][None, :]
        qg_t = jnp.tile(q_norm_g[l], nh)[None, :]
        kg_t = jnp.tile(k_norm_g[l], nh)[None, :]
        bias_rows = jnp.broadcast_to(sb_bias[l][:, None], (nh, LANES))
        bias_cols = jnp.repeat(sb_bias[l], dseq)[None, :]
        lr = ssm_lambda_re[l].reshape(1, ns)
        li = ssm_lambda_im[l].reshape(1, ns)
        ldt = jnp.broadcast_to(ssm_log_dt[l][:, None], (n_grp_ssm, n_state)).reshape(1, ns)
        bre_bd = _block_diag_rows(jnp.swapaxes(ssm_b_re[l], 1, 2))
        bim_bd = _block_diag_rows(jnp.swapaxes(ssm_b_im[l], 1, 2))
        cre_bd = _block_diag_rows(jnp.swapaxes(ssm_c_re[l], 1, 2))
        cim_bd = _block_diag_rows(jnp.swapaxes(ssm_c_im[l], 1, 2))
        prep = _ssm_prep(lr, li, ldt, bre_bd, bim_bd, cre_bd, cim_bd, t_rows)
        prep_s = tuple(prep[:3]) + tuple(jnp.tile(tb[:dseq], (db, 1)) for tb in prep[3:])
        dskip = ssm_d[l][None, :]
        glub = ssm_glu_b[l][None, :]
        pw_bd = _block_diag_rows(pool_w[l]).astype(BF16)
        pscale = pool_scale[l][None, :]
        g_ffn = norm_ffn_g[l][None, :]
        rw = jnp.zeros((d, LANES), F32).at[:, :n_exp].set(router_expert_w[l])
        rw = rw.at[:, n_exp:n_exp + n_exp_grp].set(router_group_w[l])
        rw_hi = rw.astype(BF16)
        rw_lo = (rw - rw_hi.astype(F32)).astype(BF16)
        rb = jnp.zeros((1, LANES), F32).at[0, :n_exp].set(router_expert_b[l])
        rb = rb.at[0, n_exp:n_exp + n_exp_grp].set(router_group_b[l])

        q_bf, k32, v32, k_bf, v_bf, us, up = _inproj(xp, g_mix, w_in_bf[l], qg_t, kg_t, ones_bd, tm=512)
        attn = _attn_prompt(q_bf, k_bf, v_bf, bias_rows, bsz, seq, tq=1024)
        sso, poo, hlast = _mixers_prompt(us, up, prep, l2_prompt, dskip, glu_w_bf[l], glub, pw_bd, pscale,
                                         bsz, seq, t_rows)
        x1, t_bf, gates = _outproj(xp, attn, sso, poo, w_out_bf[l], g_ffn, rw_hi, rw_lo, rb, n_exp, n_exp_grp, tm=512)
        xp = _moe(t_bf, gates, x1, wg_bf, wu_bf, wd_bf, l, tm=1024, epb=2)
        outs["kp"].append(k32.reshape(np_tok // page, page, nh, hd))
        outs["vp"].append(v32.reshape(np_tok // page, page, nh, hd))
        outs["hrp"].append(hlast[:, 0, :ns].reshape(bsz, n_grp_ssm, n_state))
        outs["hip"].append(hlast[:, 0, ns:].reshape(bsz, n_grp_ssm, n_state))
        outs["pbp"].append(up.reshape(bsz, seq, sw)[:, seq - pool_buf:, :])

        q_bf, k32, v32, k_bf, v_bf, us, up = _inproj(xs, g_mix, w_in_bf[l], qg_t, kg_t, ones_bd, tm=ns_tok)
        attn = _attn_sample(q_bf.reshape(db, dseq, aw), k32.reshape(db, dseq, aw), v32.reshape(db, dseq, aw),
                            cache_k4, cache_v4, page_table, bias_cols, l, n_pages_step=4)
        ext = jnp.concatenate([jnp.zeros((db, POOL_HIST - pool_buf, sw), F32), state_pool[l],
                               up.reshape(db, dseq, sw)], axis=1)
        h0 = jnp.concatenate([state_ssm_re[l].reshape(db, ns), state_ssm_im[l].reshape(db, ns)], axis=1)
        h0rep = jnp.repeat(h0, dseq, axis=0)
        sso, poo_ext, hall = _mixers_sample(us, ext.reshape(db * seg_rows, sw), h0rep, prep_s, l2_sample, dskip,
                                            glu_w_bf[l], glub, pw_bd, pscale, seg_rows, past_len)
        poo = poo_ext.reshape(db, seg_rows, sw)[:, POOL_HIST:, :].reshape(ns_tok, sw)
        x1, t_bf, gates = _outproj(xs, attn.reshape(ns_tok, aw), sso, poo, w_out_bf[l], g_ffn, rw_hi, rw_lo, rb,
                                   n_exp, n_exp_grp, tm=ns_tok)
        xs = _moe(t_bf, gates, x1, wg_bf, wu_bf, wd_bf, l, tm=ns_tok, epb=2)
        hl = hall.reshape(db, dseq, 2 * ns)[:, dseq - 1, :]
        outs["ks"].append(k32.reshape(db, dseq, nh, hd))
        outs["vs"].append(v32.reshape(db, dseq, nh, hd))
        outs["hrs"].append(hl[:, :ns].reshape(db, n_grp_ssm, n_state))
        outs["his"].append(hl[:, ns:].reshape(db, n_grp_ssm, n_state))
        outs["pbs"].append(ext[:, seg_rows - pool_buf:, :])

    st = {k: jnp.stack(v) for k, v in outs.items()}
    return (xp.reshape(bsz, seq, d), xs.reshape(db, dseq, d), st["kp"], st["vp"], st["hrp"], st["hip"], st["pbp"],
            st["ks"], st["vs"], st["hrs"], st["his"], st["pbs"])
```

```python
import functools
import math

import numpy as np
import jax
import jax.numpy as jnp
from jax import lax
from jax.experimental import pallas as pl
from jax.experimental.pallas import tpu as pltpu

F32 = jnp.float32
BF16 = jnp.bfloat16
I32 = jnp.int32

EPS = 1e-6
HEAD_DIM = 64
POOL_WINDOWS = (2, 4, 8, 16)
POOL_HIST = 16
TOP_K_INNER = 2
LANES = 128
VMEM_LIMIT = 56 * 1024 * 1024

NEG_INF = float("-inf")
LOG2E = math.log2(math.e)


def _cparams(*sem):
    return pltpu.CompilerParams(dimension_semantics=sem, vmem_limit_bytes=VMEM_LIMIT)


def _split_bf16(x):
    hi = x.astype(BF16)
    lo = (x - hi.astype(F32)).astype(BF16)
    return hi, lo


def _dot(a, b):
    return jnp.dot(a, b, preferred_element_type=F32)


def _dot_nt(a, b):
    return lax.dot_general(a, b, (((1,), (1,)), ((), ())), preferred_element_type=F32)


def _dot_tn(a, b):
    return lax.dot_general(a, b, (((0,), (0,)), ((), ())), preferred_element_type=F32)


def _neg_abs(z):
    bits = lax.bitcast_convert_type(z, jnp.uint32) | jnp.uint32(0x80000000)
    return lax.bitcast_convert_type(bits, F32)


def _neg_softplus2(z):
    return -(jnp.maximum(z, 0.0) + jnp.log2(1.0 + jnp.exp2(_neg_abs(z))))


def _inproj_kernel(x_ref, g_ref, w_ref, qg_ref, kg_ref, ones_ref,
                   q_ref, k_ref, v_ref, kb_ref, vb_ref, us_ref, up_ref, *, aw, sw):
    x = x_ref[...]
    ms = jnp.mean(x * x, axis=-1, keepdims=True)
    h = (x * lax.rsqrt(ms + EPS) * g_ref[...]).astype(BF16)
    proj = _dot(h, w_ref[...])
    ones_bd = ones_ref[...]

    def headnorm(t, gain):
        hi, lo = _split_bf16(t * t)
        ss = _dot(hi, ones_bd) + _dot(lo, ones_bd)
        return t * lax.rsqrt(ss * (1.0 / HEAD_DIM) + EPS) * gain

    q = headnorm(proj[:, :aw], qg_ref[...])
    k = headnorm(proj[:, aw:2 * aw], kg_ref[...])
    v = proj[:, 2 * aw:3 * aw]
    q_ref[...] = (q * (LOG2E * HEAD_DIM ** -0.5)).astype(BF16)
    k_ref[...] = k
    v_ref[...] = v
    kb_ref[...] = k.astype(BF16)
    vb_ref[...] = v.astype(BF16)
    us_ref[...] = proj[:, 3 * aw:3 * aw + sw]
    up_ref[...] = proj[:, 3 * aw + sw:]


def _inproj(x2, g_mix, w_in_bf, qg_t, kg_t, ones_bd, tm):
    n, d = x2.shape
    aw = ones_bd.shape[0]
    sw = (w_in_bf.shape[1] - 3 * aw) // 2
    row = lambda i: (i, 0)
    const = lambda i: (0, 0)
    sds = jax.ShapeDtypeStruct
    return pl.pallas_call(
        functools.partial(_inproj_kernel, aw=aw, sw=sw),
        out_shape=[sds((n, aw), BF16), sds((n, aw), F32), sds((n, aw), F32),
                   sds((n, aw), BF16), sds((n, aw), BF16), sds((n, sw), F32), sds((n, sw), F32)],
        grid=(n // tm,),
        in_specs=[pl.BlockSpec((tm, d), row), pl.BlockSpec((1, d), const),
                  pl.BlockSpec(w_in_bf.shape, const), pl.BlockSpec((1, aw), const),
                  pl.BlockSpec((1, aw), const), pl.BlockSpec((aw, aw), const)],
        out_specs=[pl.BlockSpec((tm, aw), row)] * 5 + [pl.BlockSpec((tm, sw), row)] * 2,
        compiler_params=_cparams("parallel"),
        name="inproj",
    )(x2, g_mix, w_in_bf, qg_t, kg_t, ones_bd)


def _attn_prompt_kernel(qi_ref, kj_ref, q_ref, k_ref, v_ref, uu_ref, bias_ref, o_ref,
                        r_ref, acc_ref, *, tq, tkb, sub, rc):
    s = pl.program_id(2)
    hp = pl.program_id(1)
    qi = qi_ref[s]
    kj = kj_ref[s]
    is_diag = kj == qi

    @pl.when(is_diag)
    def _():
        r_ref[...] = jnp.zeros_like(r_ref)
        acc_ref[...] = jnp.zeros_like(acc_ref)

    n_sub = tkb // sub
    n_rc = tq // rc
    lane_k = lax.broadcasted_iota(I32, (sub, LANES), 1)
    d_idx = lax.broadcasted_iota(I32, (rc, sub), 1) - lax.broadcasted_iota(I32, (rc, sub), 0)
    biases = (bias_ref[pl.ds(2 * hp, 1), :] * LOG2E, bias_ref[pl.ds(2 * hp + 1, 1), :] * LOG2E)
    uu = uu_ref[...]
    zero_bf = jnp.zeros((), BF16)
    head_lanes = (lane_k < HEAD_DIM, lane_k >= HEAD_DIM)

    def sub_block(koff, row_chunks, masked):
        kc = k_ref[pl.ds(koff, sub), :]
        vc = v_ref[pl.ds(koff, sub), :]
        k_cat = jnp.concatenate([jnp.where(m, kc, zero_bf) for m in head_lanes], axis=0)
        v_cat = jnp.concatenate([jnp.where(m, vc, zero_bf) for m in head_lanes], axis=0)
        for r in row_chunks:
            roff = r * rc
            q2 = q_ref[pl.ds(roff, rc), :]
            if masked:
                valid = d_idx < (roff - koff)
            z2 = _dot_nt(q2, k_cat)
            ps = []
            for hd in range(2):
                z = z2[:, hd * sub:(hd + 1) * sub] + biases[hd]
                sp = jnp.maximum(z, 0.0) + jnp.log2(1.0 + jnp.exp2(_neg_abs(z)))
                if masked:
                    sp = jnp.where(valid, sp, 0.0)
                hi, lo = _split_bf16(sp)
                cum = _dot(jnp.concatenate([hi, lo], axis=1), uu)
                rr = r_ref[hd, pl.ds(roff, rc), :]
                p = jnp.exp2(z - (cum + rr))
                if masked:
                    p = jnp.where(valid, p, 0.0)
                r_ref[hd, pl.ds(roff, rc), :] = rr + jnp.broadcast_to(cum[:, 0:1], rr.shape)
                ps.append(p.astype(BF16))
            acc_ref[pl.ds(roff, rc), :] += _dot(jnp.concatenate(ps, axis=1), v_cat)

    @pl.when(is_diag)
    def _():
        for c in range(n_sub - 1, -1, -1):
            sub_block(c * sub, range((c * sub) // rc, n_rc), True)

    @pl.when(jnp.logical_not(is_diag))
    def _():
        def c_body(ci, carry):
            koff = pl.multiple_of((n_sub - 1 - ci) * sub, sub)
            sub_block(koff, range(n_rc), False)
            return carry

        lax.fori_loop(0, n_sub, c_body, 0)

    @pl.when(kj == 0)
    def _():
        o_ref[...] = acc_ref[...].astype(o_ref.dtype)


def _suffix_matrix(sub):
    j = np.arange(2 * sub)[:, None] % sub
    s = np.arange(sub)[None, :]
    return jnp.asarray(j >= s, dtype=BF16)


def _attn_prompt(q_bf, k_bf, v_bf, bias_rows, bsz, seq, tq):
    n, aw = q_bf.shape
    tkb = tq
    sub = LANES
    rc = 512
    nq = seq // tq
    qi_tbl = np.concatenate([np.full(i + 1, i) for i in range(nq)]).astype(np.int32)
    kj_tbl = np.concatenate([np.arange(i, -1, -1) for i in range(nq)]).astype(np.int32)
    n_steps = qi_tbl.shape[0]
    n_hp = aw // LANES
    uu = _suffix_matrix(sub)
    grid_spec = pltpu.PrefetchScalarGridSpec(
        num_scalar_prefetch=2,
        grid=(bsz, n_hp, n_steps),
        in_specs=[
            pl.BlockSpec((tq, LANES), lambda b, h, s, qi, kj: (b * nq + qi[s], h)),
            pl.BlockSpec((tkb, LANES), lambda b, h, s, qi, kj: (b * nq + kj[s], h)),
            pl.BlockSpec((tkb, LANES), lambda b, h, s, qi, kj: (b * nq + kj[s], h)),
            pl.BlockSpec(uu.shape, lambda b, h, s, qi, kj: (0, 0)),
            pl.BlockSpec(bias_rows.shape, lambda b, h, s, qi, kj: (0, 0)),
        ],
        out_specs=pl.BlockSpec((tq, LANES), lambda b, h, s, qi, kj: (b * nq + qi[s], h)),
        scratch_shapes=[pltpu.VMEM((2, tq, LANES), F32), pltpu.VMEM((tq, LANES), F32)],
    )
    return pl.pallas_call(
        functools.partial(_attn_prompt_kernel, tq=tq, tkb=tkb, sub=sub, rc=rc),
        out_shape=jax.ShapeDtypeStruct((n, aw), BF16),
        grid_spec=grid_spec,
        compiler_params=_cparams("parallel", "parallel", "arbitrary"),
        name="attn_prompt",
    )(jnp.asarray(qi_tbl), jnp.asarray(kj_tbl), q_bf, k_bf, v_bf, uu, bias_rows)


def _attn_sample_kernel(pt_ref, q_ref, kn_ref, vn_ref, bias_ref, uu_ref, *rest, n_pages_step, page, nq, nh):
    k_refs = rest[:n_pages_step]
    v_refs = rest[n_pages_step:2 * n_pages_step]
    o_ref = rest[2 * n_pages_step]
    qbd_ref, r_ref, acc_ref = rest[2 * n_pages_step + 1:]
    del pt_ref
    j = pl.program_id(1)
    ncol = nh * nq
    aw = nh * HEAD_DIM
    col_head = lax.broadcasted_iota(I32, (ncol, aw), 0) // nq
    lane_head = lax.broadcasted_iota(I32, (ncol, aw), 1) // HEAD_DIM
    same_head = col_head == lane_head
    bias = bias_ref[...] * LOG2E

    @pl.when(j == 0)
    def _():
        q = q_ref[...].astype(F32)
        qt = jnp.concatenate([q] * nh, axis=0)
        qbd = jnp.where(same_head, qt, 0.0).astype(BF16)
        qbd_ref[...] = qbd
        pad = jnp.zeros((nq, aw), F32)
        kn = jnp.concatenate([kn_ref[...], pad], axis=0).astype(BF16)
        vn = jnp.concatenate([vn_ref[...], pad], axis=0).astype(BF16)
        z = _dot_nt(kn, qbd) + bias
        krow = lax.broadcasted_iota(I32, (2 * nq, ncol), 0)
        qcol = lax.broadcasted_iota(I32, (2 * nq, ncol), 1) % nq
        valid = krow < qcol
        lk = jnp.where(valid, _neg_softplus2(z), 0.0)
        between = jnp.zeros_like(lk)
        for jn in range(1, nq):
            between = between + jnp.where(krow < jn, lk[jn:jn + 1, :], 0.0)
        p = jnp.where(valid, jnp.exp2(z + lk + between), 0.0)
        r_ref[...] = jnp.sum(lk, axis=0, keepdims=True)
        acc_ref[...] = _dot_tn(p.astype(BF16), vn)

    kk = jnp.concatenate([r[...].astype(BF16) for r in k_refs], axis=0)
    vv = jnp.concatenate([r[...].astype(BF16) for r in v_refs], axis=0)
    z = _dot_nt(kk, qbd_ref[...]) + bias
    lk = _neg_softplus2(z)
    hi, lo = _split_bf16(lk)
    uu = uu_ref[...]
    between = _dot(uu, hi) + _dot(uu, lo)
    r_old = r_ref[...]
    p = jnp.exp2(z + lk + between + r_old)
    r_ref[...] = r_old + jnp.sum(lk, axis=0, keepdims=True)
    acc_ref[...] += _dot_tn(p.astype(BF16), vv)

    @pl.when(j == pl.num_programs(1) - 1)
    def _():
        acc = jnp.where(same_head, acc_ref[...], 0.0)
        out = acc[0:nq, :]
        for h in range(1, nh):
            out = out + acc[h * nq:(h + 1) * nq, :]
        o_ref[...] = out.astype(o_ref.dtype)


def _attn_sample(q3, kn3, vn3, cache_k4, cache_v4, page_table, bias_cols, layer, n_pages_step):
    db, nq, aw = q3.shape
    nh = aw // HEAD_DIM
    page = cache_k4.shape[2]
    n_pages = page_table.shape[1]
    n_steps = n_pages // n_pages_step
    blk = n_pages_step * page
    uu = jnp.asarray(np.triu(np.ones((blk, blk), np.float32), 1), dtype=BF16)
    ncol = nh * nq

    def page_map(i):
        return lambda b, j, pt: (layer, pt[b, (n_steps - 1 - j) * n_pages_step + i], 0, 0)

    seq_map = lambda b, j, pt: (b, 0, 0)
    const = lambda b, j, pt: (0, 0)
    page_spec = [pl.BlockSpec((None, None, page, aw), page_map(i)) for i in range(n_pages_step)]
    grid_spec = pltpu.PrefetchScalarGridSpec(
        num_scalar_prefetch=1,
        grid=(db, n_steps),
        in_specs=[pl.BlockSpec((None, nq, aw), seq_map)] * 3
        + [pl.BlockSpec((1, ncol), const), pl.BlockSpec((blk, blk), const)] + page_spec + page_spec,
        out_specs=pl.BlockSpec((None, nq, aw), seq_map),
        scratch_shapes=[pltpu.VMEM((ncol, aw), BF16), pltpu.VMEM((1, ncol), F32), pltpu.VMEM((ncol, aw), F32)],
    )
    return pl.pallas_call(
        functools.partial(_attn_sample_kernel, n_pages_step=n_pages_step, page=page, nq=nq, nh=nh),
        out_shape=jax.ShapeDtypeStruct((db, nq, aw), BF16),
        grid_spec=grid_spec,
        compiler_params=_cparams("parallel", "arbitrary"),
        name="attn_sample",
    )(page_table, q3, kn3, vn3, bias_cols, uu, *([cache_k4] * n_pages_step), *([cache_v4] * n_pages_step))


def _ssm_prep_kernel(lr_ref, li_ref, ldt_ref, bre_ref, bim_ref, cre_ref, cim_ref,
                     a_ref, bbar_ref, cbd_ref, pre_ref, pim_ref, qre_ref, qim_ref, *, t_rows):
    lr = lr_ref[...]
    li = li_ref[...]
    dt = jnp.exp(ldt_ref[...])
    mag = jnp.exp(lr * dt)
    ar = mag * jnp.cos(li * dt)
    ai = mag * jnp.sin(li * dt)
    er = ar - 1.0
    den = lr * lr + li * li
    fr = (er * lr + ai * li) / den
    fi = (ai * lr - er * li) / den
    a_ref[0:1, :] = ar
    a_ref[1:2, :] = ai
    bre = bre_ref[...]
    bim = bim_ref[...]
    ns = lr.shape[1]
    bbar_ref[:, :ns] = (fr * bre - fi * bim).astype(BF16)
    bbar_ref[:, ns:] = (fr * bim + fi * bre).astype(BF16)
    cbd_ref[:ns, :] = cre_ref[...].astype(BF16)
    cbd_ref[ns:, :] = (-cim_ref[...]).astype(BF16)
    t = lax.broadcasted_iota(I32, (t_rows, ns), 0).astype(F32)
    grow = jnp.exp(t * (lr * dt))
    shrink = jnp.exp(-t * (lr * dt))
    ang = t * (li * dt)
    c = jnp.cos(ang)
    s = jnp.sin(ang)
    pre_ref[...] = grow * c
    pim_ref[...] = grow * s
    qre_ref[...] = shrink * c
    qim_ref[...] = -(shrink * s)


def _ssm_prep(lr, li, ldt, bre_bd, bim_bd, cre_bd, cim_bd, t_rows):
    ns = lr.shape[1]
    sw = bre_bd.shape[0]
    sds = jax.ShapeDtypeStruct
    return pl.pallas_call(
        functools.partial(_ssm_prep_kernel, t_rows=t_rows),
        out_shape=[sds((2, ns), F32), sds((sw, 2 * ns), BF16), sds((2 * ns, sw), BF16)]
        + [sds((t_rows, ns), F32)] * 4,
        compiler_params=pltpu.CompilerParams(vmem_limit_bytes=VMEM_LIMIT),
        name="ssm_prep",
    )(lr, li, ldt, bre_bd, bim_bd, cre_bd, cim_bd)


def _ssm_compute(u32, bbar, cbd, a, pre, pim, qre, qim, l2, hp_re, hp_im, dskip, gluw, glub):
    ns = pre.shape[1]
    bu = _dot(u32.astype(BF16), bbar)
    bur, bui = bu[:, :ns], bu[:, ns:]
    xr = qre * bur - qim * bui
    xi = qre * bui + qim * bur
    x_hi, x_lo = _split_bf16(jnp.concatenate([xr, xi], axis=1))
    s = _dot(l2, jnp.concatenate([x_hi, x_lo], axis=0))
    ar, ai = a[0:1, :], a[1:2, :]
    sr = s[:, :ns] + (ar * hp_re - ai * hp_im)
    si = s[:, ns:] + (ar * hp_im + ai * hp_re)
    hr = pre * sr - pim * si
    hi_ = pre * si + pim * sr
    hcat = jnp.concatenate([hr, hi_], axis=1)
    y = _dot(hcat.astype(BF16), cbd) + dskip * u32
    g = 0.5 * y * (1.0 + jnp.tanh(math.sqrt(2.0 / math.pi) * (y + 0.044715 * (y * y * y))))
    gate = 1.0 / (1.0 + jnp.exp(-(_dot(g.astype(BF16), gluw) + glub)))
    return g * gate, hcat


def _pool_compute(ext, pos, pw_bd, pscale):
    s2 = ext + pltpu.roll(ext, 1, 0)
    s4 = s2 + pltpu.roll(s2, 2, 0)
    s8 = s4 + pltpu.roll(s4, 4, 0)
    s16 = s8 + pltpu.roll(s8, 8, 0)
    grp = lax.broadcasted_iota(I32, ext.shape, 1) // (ext.shape[1] // len(POOL_WINDOWS))
    win = jnp.where(grp == 0, s2, jnp.where(grp == 1, s4, jnp.where(grp == 2, s8, s16)))
    w = jnp.where(grp == 0, POOL_WINDOWS[0],
                  jnp.where(grp == 1, POOL_WINDOWS[1], jnp.where(grp == 2, POOL_WINDOWS[2], POOL_WINDOWS[3])))
    cnt = jnp.maximum(jnp.minimum(pos + 1, w), 1).astype(F32)
    d = win / cnt - ext
    return _dot(d.astype(BF16), pw_bd) * pscale


def _mixers_prompt_kernel(us_ref, up_ref, bbar_ref, cbd_ref, a_ref, pre_ref, pim_ref, qre_ref, qim_ref, l2_ref,
                          dskip_ref, gluw_ref, glub_ref, pw_ref, ps_ref,
                          so_ref, po_ref, hl_ref, carry_ref, ext_ref, *, t_rows):
    c = pl.program_id(1)
    ns = pre_ref.shape[1]

    @pl.when(c == 0)
    def _():
        carry_ref[...] = jnp.zeros_like(carry_ref)
        ext_ref[0:POOL_HIST, :] = jnp.zeros((POOL_HIST, ext_ref.shape[1]), F32)

    carry = carry_ref[...]
    out, hcat = _ssm_compute(us_ref[...], bbar_ref[...], cbd_ref[...], a_ref[...], pre_ref[...], pim_ref[...],
                             qre_ref[...], qim_ref[...], l2_ref[...], carry[:, :ns], carry[:, ns:],
                             dskip_ref[...], gluw_ref[...], glub_ref[...])
    so_ref[...] = out.astype(so_ref.dtype)
    last = hcat[t_rows - 1:t_rows, :]
    carry_ref[...] = last
    hl_ref[...] = last

    ext_ref[POOL_HIST:, :] = up_ref[...]
    ext = ext_ref[...]
    pos = c * t_rows - POOL_HIST + lax.broadcasted_iota(I32, (t_rows + POOL_HIST, 1), 0)
    pooled = _pool_compute(ext, pos, pw_ref[...], ps_ref[...])
    po_ref[...] = pooled[POOL_HIST:, :].astype(po_ref.dtype)
    ext_ref[0:POOL_HIST, :] = ext[t_rows:, :]


def _mixers_prompt(us, up, prep, l2, dskip, gluw, glub, pw_bd, pscale, bsz, seq, t_rows):
    a, bbar, cbd, pre, pim, qre, qim = prep
    n, sw = us.shape
    ns = pre.shape[1]
    nc = seq // t_rows
    row = lambda b, c: (b * nc + c, 0)
    const = lambda b, c: (0, 0)
    full = lambda arr: pl.BlockSpec(arr.shape, const)
    sds = jax.ShapeDtypeStruct
    return pl.pallas_call(
        functools.partial(_mixers_prompt_kernel, t_rows=t_rows),
        out_shape=[sds((n, sw), BF16), sds((n, sw), BF16), sds((bsz, 1, 2 * ns), F32)],
        grid=(bsz, nc),
        in_specs=[pl.BlockSpec((t_rows, sw), row), pl.BlockSpec((t_rows, sw), row),
                  full(bbar), full(cbd), full(a), full(pre), full(pim), full(qre), full(qim), full(l2),
                  full(dskip), full(gluw), full(glub), full(pw_bd), full(pscale)],
        out_specs=[pl.BlockSpec((t_rows, sw), row), pl.BlockSpec((t_rows, sw), row),
                   pl.BlockSpec((None, 1, 2 * ns), lambda b, c: (b, 0, 0))],
        scratch_shapes=[pltpu.VMEM((1, 2 * ns), F32), pltpu.VMEM((t_rows + POOL_HIST, sw), F32)],
        compiler_params=_cparams("parallel", "arbitrary"),
        name="mixers_prompt",
    )(us, up, bbar, cbd, a, pre, pim, qre, qim, l2, dskip, gluw, glub, pw_bd, pscale)


def _mixers_sample_kernel(us_ref, ext_ref, h0_ref, bbar_ref, cbd_ref, a_ref, pre_ref, pim_ref, qre_ref, qim_ref,
                          l2_ref, dskip_ref, gluw_ref, glub_ref, pw_ref, ps_ref,
                          so_ref, po_ref, h_ref, *, seg_rows, start_pos):
    ns = pre_ref.shape[1]
    h0 = h0_ref[...]
    out, hcat = _ssm_compute(us_ref[...], bbar_ref[...], cbd_ref[...], a_ref[...], pre_ref[...], pim_ref[...],
                             qre_ref[...], qim_ref[...], l2_ref[...], h0[:, :ns], h0[:, ns:],
                             dskip_ref[...], gluw_ref[...], glub_ref[...])
    so_ref[...] = out.astype(so_ref.dtype)
    h_ref[...] = hcat
    ext = ext_ref[...]
    pos = start_pos - POOL_HIST + lax.broadcasted_iota(I32, (ext.shape[0], 1), 0) % seg_rows
    po_ref[...] = _pool_compute(ext, pos, pw_ref[...], ps_ref[...]).astype(po_ref.dtype)


def _mixers_sample(us, ext, h0rep, prep_s, l2, dskip, gluw, glub, pw_bd, pscale, seg_rows, start_pos):
    a, bbar, cbd, pre, pim, qre, qim = prep_s
    n, sw = us.shape
    ns = pre.shape[1]
    sds = jax.ShapeDtypeStruct
    return pl.pallas_call(
        functools.partial(_mixers_sample_kernel, seg_rows=seg_rows, start_pos=start_pos),
        out_shape=[sds((n, sw), BF16), sds(ext.shape, BF16), sds((n, 2 * ns), F32)],
        compiler_params=pltpu.CompilerParams(vmem_limit_bytes=VMEM_LIMIT),
        name="mixers_sample",
    )(us, ext, h0rep, bbar, cbd, a, pre, pim, qre, qim, l2, dskip, gluw, glub, pw_bd, pscale)


def _outproj_kernel(x_ref, at_ref, so_ref, po_ref, wo_ref, g_ref, rwh_ref, rwl_ref, rb_ref,
                    x1_ref, t_ref, gates_ref, *, n_exp, n_grp):
    mix = jnp.concatenate([at_ref[...], so_ref[...], po_ref[...]], axis=1)
    x1 = x_ref[...] + _dot(mix, wo_ref[...])
    x1_ref[...] = x1
    ms = jnp.mean(x1 * x1, axis=-1, keepdims=True)
    t = x1 * lax.rsqrt(ms + EPS) * g_ref[...]
    t_ref[...] = t.astype(BF16)
    t_hi, t_lo = _split_bf16(t)
    rwh = rwh_ref[...]
    logits = _dot(t_hi, rwh) + _dot(t_hi, rwl_ref[...]) + _dot(t_lo, rwh) + rb_ref[...]
    lane = lax.broadcasted_iota(I32, logits.shape, 1)
    big = jnp.int32(2 ** 30)
    epg = n_exp // n_grp
    is_g = (lane >= n_exp) & (lane < n_exp + n_grp)
    gl = jnp.where(is_g, logits, NEG_INF)
    gmax = jnp.max(gl, axis=1, keepdims=True)
    gidx = jnp.min(jnp.where(gl == gmax, lane - n_exp, big), axis=1, keepdims=True)
    g_w = 1.0 / jnp.sum(jnp.where(is_g, jnp.exp(gl - gmax), 0.0), axis=1, keepdims=True)
    in_grp = (lane < n_exp) & (lane // epg == gidx)
    el = jnp.where(in_grp, logits, NEG_INF)
    m1 = jnp.max(el, axis=1, keepdims=True)
    i1 = jnp.min(jnp.where(el == m1, lane, big), axis=1, keepdims=True)
    el2 = jnp.where(lane == i1, NEG_INF, el)
    m2 = jnp.max(el2, axis=1, keepdims=True)
    i2 = jnp.min(jnp.where(el2 == m2, lane, big), axis=1, keepdims=True)
    e21 = jnp.exp(m2 - m1)
    w1 = 1.0 / (1.0 + e21)
    w2 = e21 * w1
    gates_ref[...] = jnp.where(lane == i1, w1 * g_w, jnp.where(lane == i2, w2 * g_w, 0.0))


def _outproj(x2, attn, sso, poo, wo_bf, g_ffn, rw_hi, rw_lo, rb, n_exp, n_grp, tm):
    n, d = x2.shape
    aw, sw = attn.shape[1], sso.shape[1]
    row = lambda i: (i, 0)
    const = lambda i: (0, 0)
    sds = jax.ShapeDtypeStruct
    return pl.pallas_call(
        functools.partial(_outproj_kernel, n_exp=n_exp, n_grp=n_grp),
        out_shape=[sds((n, d), F32), sds((n, d), BF16), sds((n, LANES), F32)],
        grid=(n // tm,),
        in_specs=[pl.BlockSpec((tm, d), row), pl.BlockSpec((tm, aw), row), pl.BlockSpec((tm, sw), row),
                  pl.BlockSpec((tm, sw), row), pl.BlockSpec(wo_bf.shape, const), pl.BlockSpec((1, d), const),
                  pl.BlockSpec(rw_hi.shape, const), pl.BlockSpec(rw_lo.shape, const), pl.BlockSpec((1, LANES), const)],
        out_specs=[pl.BlockSpec((tm, d), row), pl.BlockSpec((tm, d), row), pl.BlockSpec((tm, LANES), row)],
        compiler_params=_cparams("parallel"),
        name="outproj",
    )(x2, attn, sso, poo, wo_bf, g_ffn, rw_hi, rw_lo, rb)


def _moe_kernel(t_ref, gates_ref, x1_ref, wg_ref, wu_ref, wd_ref, o_ref, *, epb):
    j = pl.program_id(1)

    @pl.when(j == 0)
    def _():
        o_ref[...] = x1_ref[...]

    t = t_ref[...]
    g_hi, g_lo = _split_bf16(gates_ref[...])
    g2 = jnp.concatenate([g_hi, g_lo], axis=1)
    dh = wg_ref.shape[2]
    sel_row = lax.broadcasted_iota(I32, (2 * LANES, dh), 0) % LANES
    acc = jnp.zeros(o_ref.shape, F32)
    for i in range(epb):
        e = j * epb + i
        gb = _dot(g2, (sel_row == e).astype(BF16))
        a = _dot(t, wg_ref[i])
        b = _dot(t, wu_ref[i])
        hh = (a / (1.0 + jnp.exp(-a))) * b * gb
        acc = acc + _dot(hh.astype(BF16), wd_ref[i])
    o_ref[...] += acc


def _moe(t_bf, gates, x1, wg, wu, wd, layer, tm, epb):
    n, d = x1.shape
    n_exp, _, dh = wg.shape[1:]
    row = lambda i, j: (i, 0)
    return pl.pallas_call(
        functools.partial(_moe_kernel, epb=epb),
        out_shape=jax.ShapeDtypeStruct((n, d), F32),
        grid=(n // tm, n_exp // epb),
        in_specs=[pl.BlockSpec((tm, d), row), pl.BlockSpec((tm, LANES), row), pl.BlockSpec((tm, d), row),
                  pl.BlockSpec((None, epb, d, dh), lambda i, j: (layer, j, 0, 0)),
                  pl.BlockSpec((None, epb, d, dh), lambda i, j: (layer, j, 0, 0)),
                  pl.BlockSpec((None, epb, dh, d), lambda i, j: (layer, j, 0, 0))],
        out_specs=pl.BlockSpec((tm, d), row),
        compiler_params=_cparams("parallel", "arbitrary"),
        name="moe",
    )(t_bf, gates, x1, wg, wu, wd)


def _block_diag_rows(w):
    g, a, b = w.shape
    eye = jnp.eye(g, dtype=w.dtype)
    return jnp.einsum('gab,gh->gahb', w, eye).reshape(g * a, g * b)


def _seg_cumsum_matrix(rows, seg):
    i = np.arange(rows)[:, None]
    j = np.arange(rows)[None, :]
    l = (j <= i) & (i // seg == j // seg)
    return jnp.asarray(np.concatenate([l, l], axis=1), dtype=BF16)


def kernel(x_prompt, x_sample, cache_k, cache_v, state_ssm_re, state_ssm_im, state_pool, page_table, norm_mix_g, w_in, q_norm_g, k_norm_g, sb_bias, ssm_lambda_re, ssm_lambda_im, ssm_log_dt, ssm_b_re, ssm_b_im, ssm_c_re, ssm_c_im, ssm_d, ssm_glu_w, ssm_glu_b, pool_w, pool_scale, w_out, norm_ffn_g, router_group_w, router_group_b, router_expert_w, router_expert_b, expert_w_gate, expert_w_up, expert_w_down):
    depth = w_in.shape[0]
    bsz, seq, d = x_prompt.shape
    db, dseq, _ = x_sample.shape
    n_phys, page, nh, hd = cache_k.shape[1:]
    assert hd == HEAD_DIM
    aw = nh * hd
    n_grp_ssm, n_state, grp_ch = ssm_b_re.shape[1:]
    sw = n_grp_ssm * grp_ch
    ns = n_grp_ssm * n_state
    pool_buf = state_pool.shape[2]
    assert pool_buf < POOL_HIST and max(POOL_WINDOWS) - 1 <= pool_buf
    n_exp_grp = router_group_w.shape[2]
    n_exp = router_expert_w.shape[2]
    past_len = page_table.shape[1] * page
    np_tok = bsz * seq
    ns_tok = db * dseq
    t_rows = 128

    w_in_bf = w_in.astype(BF16)
    w_out_bf = w_out.astype(BF16)
    wg_bf = expert_w_gate.astype(BF16)
    wu_bf = expert_w_up.astype(BF16)
    wd_bf = expert_w_down.astype(BF16)
    glu_w_bf = ssm_glu_w.astype(BF16)
    ones_bd = jnp.asarray(np.kron(np.eye(nh), np.ones((hd, hd))), dtype=BF16)
    cache_k4 = cache_k.reshape(depth, n_phys, page, aw)
    cache_v4 = cache_v.reshape(depth, n_phys, page, aw)
    l2_prompt = _seg_cumsum_matrix(t_rows, t_rows)
    l2_sample = _seg_cumsum_matrix(ns_tok, dseq)
    seg_rows = POOL_HIST + dseq

    xp = x_prompt.reshape(np_tok, d)
    xs = x_sample.reshape(ns_tok, d)
    outs = {k: [] for k in ("kp", "vp", "hrp", "hip", "pbp", "ks", "vs", "hrs", "his", "pbs")}

    for l in range(depth):
        g_mix = norm_mix_g[l][None, :]
        qg_t = jnp.tile(q_norm_g[l], nh)[None, :]
        kg_t = jnp.tile(k_norm_g[l], nh)[None, :]
        bias_rows = jnp.broadcast_to(sb_bias[l][:, None], (nh, LANES))
        bias_cols = jnp.repeat(sb_bias[l], dseq)[None, :]
        lr = ssm_lambda_re[l].reshape(1, ns)
        li = ssm_lambda_im[l].reshape(1, ns)
        ldt = jnp.broadcast_to(ssm_log_dt[l][:, None], (n_grp_ssm, n_state)).reshape(1, ns)
        bre_bd = _block_diag_rows(jnp.swapaxes(ssm_b_re[l], 1, 2))
        bim_bd = _block_diag_rows(jnp.swapaxes(ssm_b_im[l], 1, 2))
        cre_bd = _block_diag_rows(jnp.swapaxes(ssm_c_re[l], 1, 2))
        cim_bd = _block_diag_rows(jnp.swapaxes(ssm_c_im[l], 1, 2))
        prep = _ssm_prep(lr, li, ldt, bre_bd, bim_bd, cre_bd, cim_bd, t_rows)
        prep_s = tuple(prep[:3]) + tuple(jnp.tile(tb[:dseq], (db, 1)) for tb in prep[3:])
        dskip = ssm_d[l][None, :]
        glub = ssm_glu_b[l][None, :]
        pw_bd = _block_diag_rows(pool_w[l]).astype(BF16)
        pscale = pool_scale[l][None, :]
        g_ffn = norm_ffn_g[l][None, :]
        rw = jnp.zeros((d, LANES), F32).at[:, :n_exp].set(router_expert_w[l])
        rw = rw.at[:, n_exp:n_exp + n_exp_grp].set(router_group_w[l])
        rw_hi = rw.astype(BF16)
        rw_lo = (rw - rw_hi.astype(F32)).astype(BF16)
        rb = jnp.zeros((1, LANES), F32).at[0, :n_exp].set(router_expert_b[l])
        rb = rb.at[0, n_exp:n_exp + n_exp_grp].set(router_group_b[l])

        q_bf, k32, v32, k_bf, v_bf, us, up = _inproj(xp, g_mix, w_in_bf[l], qg_t, kg_t, ones_bd, tm=512)
        attn = _attn_prompt(q_bf, k_bf, v_bf, bias_rows, bsz, seq, tq=1024)
        sso, poo, hlast = _mixers_prompt(us, up, prep, l2_prompt, dskip, glu_w_bf[l], glub, pw_bd, pscale,
                                         bsz, seq, t_rows)
        x1, t_bf, gates = _outproj(xp, attn, sso, poo, w_out_bf[l], g_ffn, rw_hi, rw_lo, rb, n_exp, n_exp_grp, tm=512)
        xp = _moe(t_bf, gates, x1, wg_bf, wu_bf, wd_bf, l, tm=1024, epb=2)
        outs["kp"].append(k32.reshape(np_tok // page, page, nh, hd))
        outs["vp"].append(v32.reshape(np_tok // page, page, nh, hd))
        outs["hrp"].append(hlast[:, 0, :ns].reshape(bsz, n_grp_ssm, n_state))
        outs["hip"].append(hlast[:, 0, ns:].reshape(bsz, n_grp_ssm, n_state))
        outs["pbp"].append(up.reshape(bsz, seq, sw)[:, seq - pool_buf:, :])

        q_bf, k32, v32, k_bf, v_bf, us, up = _inproj(xs, g_mix, w_in_bf[l], qg_t, kg_t, ones_bd, tm=ns_tok)
        attn = _attn_sample(q_bf.reshape(db, dseq, aw), k32.reshape(db, dseq, aw), v32.reshape(db, dseq, aw),
                            cache_k4, cache_v4, page_table, bias_cols, l, n_pages_step=4)
        ext = jnp.concatenate([jnp.zeros((db, POOL_HIST - pool_buf, sw), F32), state_pool[l],
                               up.reshape(db, dseq, sw)], axis=1)
        h0 = jnp.concatenate([state_ssm_re[l].reshape(db, ns), state_ssm_im[l].reshape(db, ns)], axis=1)
        h0rep = jnp.repeat(h0, dseq, axis=0)
        sso, poo_ext, hall = _mixers_sample(us, ext.reshape(db * seg_rows, sw), h0rep, prep_s, l2_sample, dskip,
                                            glu_w_bf[l], glub, pw_bd, pscale, seg_rows, past_len)
        poo = poo_ext.reshape(db, seg_rows, sw)[:, POOL_HIST:, :].reshape(ns_tok, sw)
        x1, t_bf, gates = _outproj(xs, attn.reshape(ns_tok, aw), sso, poo, w_out_bf[l], g_ffn, rw_hi, rw_lo, rb,
                                   n_exp, n_exp_grp, tm=ns_tok)
        xs = _moe(t_bf, gates, x1, wg_bf, wu_bf, wd_bf, l, tm=ns_tok, epb=2)
        hl = hall.reshape(db, dseq, 2 * ns)[:, dseq - 1, :]
        outs["ks"].append(k32.reshape(db, dseq, nh, hd))
        outs["vs"].append(v32.reshape(db, dseq, nh, hd))
        outs["hrs"].append(hl[:, :ns].reshape(db, n_grp_ssm, n_state))
        outs["his"].append(hl[:, ns:].reshape(db, n_grp_ssm, n_state))
        outs["pbs"].append(ext[:, seg_rows - pool_buf:, :])

    st = {k: jnp.stack(v) for k, v in outs.items()}
    return (xp.reshape(bsz, seq, d), xs.reshape(db, dseq, d), st["kp"], st["vp"], st["hrp"], st["hip"], st["pbp"],
            st["ks"], st["vs"], st["hrs"], st["his"], st["pbs"])
```

```python
import functools
import math

import numpy as np
import jax
import jax.numpy as jnp
from jax import lax
from jax.experimental import pallas as pl
from jax.experimental.pallas import tpu as pltpu

F32 = jnp.float32
BF16 = jnp.bfloat16
I32 = jnp.int32

EPS = 1e-6
HEAD_DIM = 64
POOL_WINDOWS = (2, 4, 8, 16)
POOL_HIST = 16
TOP_K_INNER = 2
LANES = 128
VMEM_LIMIT = 56 * 1024 * 1024

NEG_INF = float("-inf")
LOG2E = math.log2(math.e)


def _cparams(*sem):
    return pltpu.CompilerParams(dimension_semantics=sem, vmem_limit_bytes=VMEM_LIMIT)


def _split_bf16(x):
    hi = x.astype(BF16)
    lo = (x - hi.astype(F32)).astype(BF16)
    return hi, lo


def _dot(a, b):
    return jnp.dot(a, b, preferred_element_type=F32)


def _dot_nt(a, b):
    return lax.dot_general(a, b, (((1,), (1,)), ((), ())), preferred_element_type=F32)


def _dot_tn(a, b):
    return lax.dot_general(a, b, (((0,), (0,)), ((), ())), preferred_element_type=F32)


def _neg_abs(z):
    bits = lax.bitcast_convert_type(z, jnp.uint32) | jnp.uint32(0x80000000)
    return lax.bitcast_convert_type(bits, F32)


def _neg_softplus2(z):
    return -(jnp.maximum(z, 0.0) + jnp.log2(1.0 + jnp.exp2(_neg_abs(z))))


def _inproj_kernel(x_ref, g_ref, w_ref, qg_ref, kg_ref, ones_ref,
                   q_ref, k_ref, v_ref, kb_ref, vb_ref, us_ref, up_ref, *, aw, sw):
    x = x_ref[...]
    ms = jnp.mean(x * x, axis=-1, keepdims=True)
    h = (x * lax.rsqrt(ms + EPS) * g_ref[...]).astype(BF16)
    proj = _dot(h, w_ref[...])
    ones_bd = ones_ref[...]

    def headnorm(t, gain):
        hi, lo = _split_bf16(t * t)
        ss = _dot(hi, ones_bd) + _dot(lo, ones_bd)
        return t * lax.rsqrt(ss * (1.0 / HEAD_DIM) + EPS) * gain

    q = headnorm(proj[:, :aw], qg_ref[...])
    k = headnorm(proj[:, aw:2 * aw], kg_ref[...])
    v = proj[:, 2 * aw:3 * aw]
    q_ref[...] = (q * (LOG2E * HEAD_DIM ** -0.5)).astype(BF16)
    k_ref[...] = k
    v_ref[...] = v
    kb_ref[...] = k.astype(BF16)
    vb_ref[...] = v.astype(BF16)
    us_ref[...] = proj[:, 3 * aw:3 * aw + sw]
    up_ref[...] = proj[:, 3 * aw + sw:]


def _inproj(x2, g_mix, w_in_bf, qg_t, kg_t, ones_bd, tm):
    n, d = x2.shape
    aw = ones_bd.shape[0]
    sw = (w_in_bf.shape[1] - 3 * aw) // 2
    row = lambda i: (i, 0)
    const = lambda i: (0, 0)
    sds = jax.ShapeDtypeStruct
    return pl.pallas_call(
        functools.partial(_inproj_kernel, aw=aw, sw=sw),
        out_shape=[sds((n, aw), BF16), sds((n, aw), F32), sds((n, aw), F32),
                   sds((n, aw), BF16), sds((n, aw), BF16), sds((n, sw), F32), sds((n, sw), F32)],
        grid=(n // tm,),
        in_specs=[pl.BlockSpec((tm, d), row), pl.BlockSpec((1, d), const),
                  pl.BlockSpec(w_in_bf.shape, const), pl.BlockSpec((1, aw), const),
                  pl.BlockSpec((1, aw), const), pl.BlockSpec((aw, aw), const)],
        out_specs=[pl.BlockSpec((tm, aw), row)] * 5 + [pl.BlockSpec((tm, sw), row)] * 2,
        compiler_params=_cparams("parallel"),
        name="inproj",
    )(x2, g_mix, w_in_bf, qg_t, kg_t, ones_bd)


def _attn_prompt_kernel(qi_ref, kj_ref, q_ref, k_ref, v_ref, uu_ref, bias_ref, o_ref,
                        r_ref, acc_ref, *, tq, tkb, sub, rc):
    s = pl.program_id(2)
    hp = pl.program_id(1)
    qi = qi_ref[s]
    kj = kj_ref[s]
    is_diag = kj == qi

    @pl.when(is_diag)
    def _():
        r_ref[...] = jnp.zeros_like(r_ref)
        acc_ref[...] = jnp.zeros_like(acc_ref)

    n_sub = tkb // sub
    n_rc = tq // rc
    lane_k = lax.broadcasted_iota(I32, (sub, LANES), 1)
    d_idx = lax.broadcasted_iota(I32, (rc, sub), 1) - lax.broadcasted_iota(I32, (rc, sub), 0)
    biases = (bias_ref[pl.ds(2 * hp, 1), :] * LOG2E, bias_ref[pl.ds(2 * hp + 1, 1), :] * LOG2E)
    uu = uu_ref[...]
    zero_bf = jnp.zeros((), BF16)
    head_lanes = (lane_k < HEAD_DIM, lane_k >= HEAD_DIM)

    def run(tiles, masked):
        kv = {}

        def kv_cat(c):
            if c not in kv:
                kc = k_ref[pl.ds(c * sub, sub), :]
                vc = v_ref[pl.ds(c * sub, sub), :]
                kv[c] = (jnp.concatenate([jnp.where(m, kc, zero_bf) for m in head_lanes], axis=0),
                         jnp.concatenate([jnp.where(m, vc, zero_bf) for m in head_lanes], axis=0))
            return kv[c]

        def scores(c, r):
            return _dot_nt(q_ref[pl.ds(r * rc, rc), :], kv_cat(c)[0])

        def suffix_sums(c, r, z2):
            valid = d_idx < (r * rc - c * sub) if masked else None
            zs, cums = [], []
            for hd in range(2):
                z = z2[:, hd * sub:(hd + 1) * sub] + biases[hd]
                sp = jnp.maximum(z, 0.0) + jnp.log2(1.0 + jnp.exp2(_neg_abs(z)))
                if masked:
                    sp = jnp.where(valid, sp, 0.0)
                zs.append(z)
                cums.append(_dot(sp.astype(BF16), uu[:sub]))
            return zs, cums, valid

        def weights_and_values(c, r, zs, cums, valid):
            rows = pl.ds(r * rc, rc)
            ps = []
            for hd in range(2):
                rr = r_ref[hd, rows, :]
                p = jnp.exp2(zs[hd] - (cums[hd] + rr))
                if masked:
                    p = jnp.where(valid, p, 0.0)
                r_ref[hd, rows, :] = rr + jnp.broadcast_to(cums[hd][:, 0:1], rr.shape)
                ps.append(p.astype(BF16))
            acc_ref[rows, :] += _dot(jnp.concatenate(ps, axis=1), kv_cat(c)[1])

        z2 = scores(*tiles[0])
        for n, (c, r) in enumerate(tiles):
            mid = suffix_sums(c, r, z2)
            if n + 1 < len(tiles):
                z2 = scores(*tiles[n + 1])
            weights_and_values(c, r, *mid)

    @pl.when(is_diag)
    def _():
        run([(c, r) for c in range(n_sub - 1, -1, -1) for r in range((c * sub) // rc, n_rc)], True)

    @pl.when(jnp.logical_not(is_diag))
    def _():
        run([(c, r) for c in range(n_sub - 1, -1, -1) for r in range(n_rc)], False)

    @pl.when(kj == 0)
    def _():
        o_ref[...] = acc_ref[...].astype(o_ref.dtype)


def _suffix_matrix(sub):
    j = np.arange(2 * sub)[:, None] % sub
    s = np.arange(sub)[None, :]
    return jnp.asarray(j >= s, dtype=BF16)


def _attn_prompt(q_bf, k_bf, v_bf, bias_rows, bsz, seq, tq):
    n, aw = q_bf.shape
    tkb = tq
    sub = LANES
    rc = 512
    nq = seq // tq
    qi_tbl = np.concatenate([np.full(i + 1, i) for i in range(nq)]).astype(np.int32)
    kj_tbl = np.concatenate([np.arange(i, -1, -1) for i in range(nq)]).astype(np.int32)
    n_steps = qi_tbl.shape[0]
    n_hp = aw // LANES
    uu = _suffix_matrix(sub)
    grid_spec = pltpu.PrefetchScalarGridSpec(
        num_scalar_prefetch=2,
        grid=(bsz, n_hp, n_steps),
        in_specs=[
            pl.BlockSpec((tq, LANES), lambda b, h, s, qi, kj: (b * nq + qi[s], h)),
            pl.BlockSpec((tkb, LANES), lambda b, h, s, qi, kj: (b * nq + kj[s], h)),
            pl.BlockSpec((tkb, LANES), lambda b, h, s, qi, kj: (b * nq + kj[s], h)),
            pl.BlockSpec(uu.shape, lambda b, h, s, qi, kj: (0, 0)),
            pl.BlockSpec(bias_rows.shape, lambda b, h, s, qi, kj: (0, 0)),
        ],
        out_specs=pl.BlockSpec((tq, LANES), lambda b, h, s, qi, kj: (b * nq + qi[s], h)),
        scratch_shapes=[pltpu.VMEM((2, tq, LANES), F32), pltpu.VMEM((tq, LANES), F32)],
    )
    return pl.pallas_call(
        functools.partial(_attn_prompt_kernel, tq=tq, tkb=tkb, sub=sub, rc=rc),
        out_shape=jax.ShapeDtypeStruct((n, aw), BF16),
        grid_spec=grid_spec,
        compiler_params=_cparams("parallel", "parallel", "arbitrary"),
        name="attn_prompt",
    )(jnp.asarray(qi_tbl), jnp.asarray(kj_tbl), q_bf, k_bf, v_bf, uu, bias_rows)


def _attn_sample_kernel(pt_ref, q_ref, kn_ref, vn_ref, bias_ref, uu_ref, *rest, n_pages_step, nq, nh):
    k_refs = rest[:n_pages_step]
    v_refs = rest[n_pages_step:2 * n_pages_step]
    o_ref = rest[2 * n_pages_step]
    r_ref, acc_ref = rest[2 * n_pages_step + 1:]
    del pt_ref
    j = pl.program_id(1)
    nrow = nh * nq
    bias = bias_ref[...] * LOG2E
    qbd = q_ref[...]
    uu = uu_ref[...]

    def softplus2(z):
        return jnp.maximum(z, 0.0) + jnp.log2(1.0 + jnp.exp2(_neg_abs(z)))

    @pl.when(j == 0)
    def _():
        z = _dot_nt(qbd, kn_ref[...].astype(BF16)) + bias
        lane_tok = lax.broadcasted_iota(I32, (nrow, LANES), 1)
        row_q = lax.broadcasted_iota(I32, (nrow, LANES), 0) % nq
        valid = lane_tok < row_q
        hi, lo = _split_bf16(jnp.where(valid, softplus2(z), 0.0))
        ct = _dot(jnp.concatenate([hi, lo], axis=1), uu)
        p = jnp.where(valid, jnp.exp2(z - ct[:, :LANES]), 0.0)
        r_ref[...] = ct[:, LANES:]
        acc_ref[...] = _dot(p.astype(BF16), vn_ref[...].astype(BF16))

    pages = range(n_pages_step - 1, -1, -1)
    zz = {i: _dot(qbd, k_refs[i][...].astype(BF16)) + bias for i in pages}
    hi, lo = _split_bf16(jnp.concatenate([softplus2(zz[i]) for i in pages], axis=0))
    ct = _dot(jnp.concatenate([hi, lo], axis=1), uu)
    rr = r_ref[...]
    ps = {}
    for n, i in enumerate(pages):
        ps[i] = jnp.exp2(zz[i] - (ct[n * nrow:(n + 1) * nrow, :LANES] + rr)).astype(BF16)
        rr = rr + ct[n * nrow:(n + 1) * nrow, LANES:]
    acc = acc_ref[...]
    for i in pages:
        acc = acc + _dot_nt(ps[i], v_refs[i][...].astype(BF16))
    r_ref[...] = rr
    acc_ref[...] = acc

    @pl.when(j == pl.num_programs(1) - 1)
    def _():
        row_head = lax.broadcasted_iota(I32, acc.shape, 0) // nq
        col_head = lax.broadcasted_iota(I32, acc.shape, 1) // HEAD_DIM
        own = jnp.where(row_head == col_head, acc, 0.0)
        out = own[0:nq, :]
        for h in range(1, nh):
            out = out + own[h * nq:(h + 1) * nq, :]
        o_ref[...] = out.astype(o_ref.dtype)


def _attn_sample(qbd, kn_pad, vn_pad, cache_kt, cache_vt, page_table, bias_rows, layer, n_pages_step, nq):
    db, nrow, aw = qbd.shape
    nh = nrow // nq
    page = cache_kt.shape[3]
    assert page == LANES
    n_pages = page_table.shape[1]
    n_steps = n_pages // n_pages_step
    jj = np.arange(2 * LANES)[:, None] % LANES
    ss = np.arange(2 * LANES)[None, :]
    uu = jnp.asarray((ss >= LANES) | (jj >= ss), dtype=BF16)

    def page_map(i):
        return lambda b, j, pt: (layer, pt[b, (n_steps - 1 - j) * n_pages_step + i], 0, 0)

    seq_map = lambda b, j, pt: (b, 0, 0)
    const = lambda b, j, pt: (0, 0)
    page_spec = [pl.BlockSpec((None, None, aw, page), page_map(i)) for i in range(n_pages_step)]
    grid_spec = pltpu.PrefetchScalarGridSpec(
        num_scalar_prefetch=1,
        grid=(db, n_steps),
        in_specs=[pl.BlockSpec((None, nrow, aw), seq_map), pl.BlockSpec((None, LANES, aw), seq_map),
                  pl.BlockSpec((None, LANES, aw), seq_map), pl.BlockSpec(bias_rows.shape, const),
                  pl.BlockSpec(uu.shape, const)] + page_spec + page_spec,
        out_specs=pl.BlockSpec((None, nq, aw), seq_map),
        scratch_shapes=[pltpu.VMEM((nrow, LANES), F32), pltpu.VMEM((nrow, aw), F32)],
    )
    return pl.pallas_call(
        functools.partial(_attn_sample_kernel, n_pages_step=n_pages_step, nq=nq, nh=nh),
        out_shape=jax.ShapeDtypeStruct((db, nq, aw), BF16),
        grid_spec=grid_spec,
        compiler_params=_cparams("parallel", "arbitrary"),
        name="attn_sample",
    )(page_table, qbd, kn_pad, vn_pad, bias_rows, uu, *([cache_kt] * n_pages_step), *([cache_vt] * n_pages_step))


def _ssm_prep_kernel(lr_ref, li_ref, ldt_ref, bre_ref, bim_ref, cre_ref, cim_ref,
                     a_ref, bbar_ref, cbd_ref, pre_ref, pim_ref, qre_ref, qim_ref, *, t_rows):
    lr = lr_ref[...]
    li = li_ref[...]
    dt = jnp.exp(ldt_ref[...])
    mag = jnp.exp(lr * dt)
    ar = mag * jnp.cos(li * dt)
    ai = mag * jnp.sin(li * dt)
    er = ar - 1.0
    den = lr * lr + li * li
    fr = (er * lr + ai * li) / den
    fi = (ai * lr - er * li) / den
    a_ref[0:1, :] = ar
    a_ref[1:2, :] = ai
    bre = bre_ref[...]
    bim = bim_ref[...]
    ns = lr.shape[1]
    bbar_ref[:, :ns] = (fr * bre - fi * bim).astype(BF16)
    bbar_ref[:, ns:] = (fr * bim + fi * bre).astype(BF16)
    cbd_ref[:ns, :] = cre_ref[...].astype(BF16)
    cbd_ref[ns:, :] = (-cim_ref[...]).astype(BF16)
    t = lax.broadcasted_iota(I32, (t_rows, ns), 0).astype(F32)
    grow = jnp.exp(t * (lr * dt))
    shrink = jnp.exp(-t * (lr * dt))
    ang = t * (li * dt)
    c = jnp.cos(ang)
    s = jnp.sin(ang)
    pre_ref[...] = grow * c
    pim_ref[...] = grow * s
    qre_ref[...] = shrink * c
    qim_ref[...] = -(shrink * s)


def _ssm_prep(lr, li, ldt, bre_bd, bim_bd, cre_bd, cim_bd, t_rows):
    ns = lr.shape[1]
    sw = bre_bd.shape[0]
    sds = jax.ShapeDtypeStruct
    return pl.pallas_call(
        functools.partial(_ssm_prep_kernel, t_rows=t_rows),
        out_shape=[sds((2, ns), F32), sds((sw, 2 * ns), BF16), sds((2 * ns, sw), BF16)]
        + [sds((t_rows, ns), F32)] * 4,
        compiler_params=pltpu.CompilerParams(vmem_limit_bytes=VMEM_LIMIT),
        name="ssm_prep",
    )(lr, li, ldt, bre_bd, bim_bd, cre_bd, cim_bd)


def _ssm_compute(u32, bbar, cbd, a, pre, pim, qre, qim, l2, hp_re, hp_im, dskip, gluw, glub):
    ns = pre.shape[1]
    bu = _dot(u32.astype(BF16), bbar)
    bur, bui = bu[:, :ns], bu[:, ns:]
    xr = qre * bur - qim * bui
    xi = qre * bui + qim * bur
    x_hi, x_lo = _split_bf16(jnp.concatenate([xr, xi], axis=1))
    s = _dot(l2, jnp.concatenate([x_hi, x_lo], axis=0))
    ar, ai = a[0:1, :], a[1:2, :]
    sr = s[:, :ns] + (ar * hp_re - ai * hp_im)
    si = s[:, ns:] + (ar * hp_im + ai * hp_re)
    hr = pre * sr - pim * si
    hi_ = pre * si + pim * sr
    hcat = jnp.concatenate([hr, hi_], axis=1)
    y = _dot(hcat.astype(BF16), cbd) + dskip * u32
    g = 0.5 * y * (1.0 + jnp.tanh(math.sqrt(2.0 / math.pi) * (y + 0.044715 * (y * y * y))))
    gate = 1.0 / (1.0 + jnp.exp(-(_dot(g.astype(BF16), gluw) + glub)))
    return g * gate, hcat


def _pool_compute(ext, pos, pw_bd, pscale):
    s2 = ext + pltpu.roll(ext, 1, 0)
    s4 = s2 + pltpu.roll(s2, 2, 0)
    s8 = s4 + pltpu.roll(s4, 4, 0)
    s16 = s8 + pltpu.roll(s8, 8, 0)
    grp = lax.broadcasted_iota(I32, ext.shape, 1) // (ext.shape[1] // len(POOL_WINDOWS))
    win = jnp.where(grp == 0, s2, jnp.where(grp == 1, s4, jnp.where(grp == 2, s8, s16)))
    w = jnp.where(grp == 0, POOL_WINDOWS[0],
                  jnp.where(grp == 1, POOL_WINDOWS[1], jnp.where(grp == 2, POOL_WINDOWS[2], POOL_WINDOWS[3])))
    cnt = jnp.maximum(jnp.minimum(pos + 1, w), 1).astype(F32)
    d = win / cnt - ext
    return _dot(d.astype(BF16), pw_bd) * pscale


def _mixers_prompt_kernel(us_ref, up_ref, bbar_ref, cbd_ref, a_ref, pre_ref, pim_ref, qre_ref, qim_ref, l2_ref,
                          dskip_ref, gluw_ref, glub_ref, pw_ref, ps_ref,
                          so_ref, po_ref, hl_ref, carry_ref, ext_ref, *, t_rows):
    c = pl.program_id(1)
    ns = pre_ref.shape[1]

    @pl.when(c == 0)
    def _():
        carry_ref[...] = jnp.zeros_like(carry_ref)
        ext_ref[0:POOL_HIST, :] = jnp.zeros((POOL_HIST, ext_ref.shape[1]), F32)

    carry = carry_ref[...]
    out, hcat = _ssm_compute(us_ref[...], bbar_ref[...], cbd_ref[...], a_ref[...], pre_ref[...], pim_ref[...],
                             qre_ref[...], qim_ref[...], l2_ref[...], carry[:, :ns], carry[:, ns:],
                             dskip_ref[...], gluw_ref[...], glub_ref[...])
    so_ref[...] = out.astype(so_ref.dtype)
    last = hcat[t_rows - 1:t_rows, :]
    carry_ref[...] = last
    hl_ref[...] = last

    ext_ref[POOL_HIST:, :] = up_ref[...]
    ext = ext_ref[...]
    pos = c * t_rows - POOL_HIST + lax.broadcasted_iota(I32, (t_rows + POOL_HIST, 1), 0)
    pooled = _pool_compute(ext, pos, pw_ref[...], ps_ref[...])
    po_ref[...] = pooled[POOL_HIST:, :].astype(po_ref.dtype)
    ext_ref[0:POOL_HIST, :] = ext[t_rows:, :]


def _mixers_prompt(us, up, prep, l2, dskip, gluw, glub, pw_bd, pscale, bsz, seq, t_rows):
    a, bbar, cbd, pre, pim, qre, qim = prep
    n, sw = us.shape
    ns = pre.shape[1]
    nc = seq // t_rows
    row = lambda b, c: (b * nc + c, 0)
    const = lambda b, c: (0, 0)
    full = lambda arr: pl.BlockSpec(arr.shape, const)
    sds = jax.ShapeDtypeStruct
    return pl.pallas_call(
        functools.partial(_mixers_prompt_kernel, t_rows=t_rows),
        out_shape=[sds((n, sw), BF16), sds((n, sw), BF16), sds((bsz, 1, 2 * ns), F32)],
        grid=(bsz, nc),
        in_specs=[pl.BlockSpec((t_rows, sw), row), pl.BlockSpec((t_rows, sw), row),
                  full(bbar), full(cbd), full(a), full(pre), full(pim), full(qre), full(qim), full(l2),
                  full(dskip), full(gluw), full(glub), full(pw_bd), full(pscale)],
        out_specs=[pl.BlockSpec((t_rows, sw), row), pl.BlockSpec((t_rows, sw), row),
                   pl.BlockSpec((None, 1, 2 * ns), lambda b, c: (b, 0, 0))],
        scratch_shapes=[pltpu.VMEM((1, 2 * ns), F32), pltpu.VMEM((t_rows + POOL_HIST, sw), F32)],
        compiler_params=_cparams("parallel", "arbitrary"),
        name="mixers_prompt",
    )(us, up, bbar, cbd, a, pre, pim, qre, qim, l2, dskip, gluw, glub, pw_bd, pscale)


def _mixers_sample_kernel(us_ref, ext_ref, h0_ref, bbar_ref, cbd_ref, a_ref, pre_ref, pim_ref, qre_ref, qim_ref,
                          l2_ref, dskip_ref, gluw_ref, glub_ref, pw_ref, ps_ref,
                          so_ref, po_ref, h_ref, *, seg_rows, start_pos):
    ns = pre_ref.shape[1]
    h0 = h0_ref[...]
    out, hcat = _ssm_compute(us_ref[...], bbar_ref[...], cbd_ref[...], a_ref[...], pre_ref[...], pim_ref[...],
                             qre_ref[...], qim_ref[...], l2_ref[...], h0[:, :ns], h0[:, ns:],
                             dskip_ref[...], gluw_ref[...], glub_ref[...])
    so_ref[...] = out.astype(so_ref.dtype)
    h_ref[...] = hcat
    ext = ext_ref[...]
    pos = start_pos - POOL_HIST + lax.broadcasted_iota(I32, (ext.shape[0], 1), 0) % seg_rows
    po_ref[...] = _pool_compute(ext, pos, pw_ref[...], ps_ref[...]).astype(po_ref.dtype)


def _mixers_sample(us, ext, h0rep, prep_s, l2, dskip, gluw, glub, pw_bd, pscale, seg_rows, start_pos):
    a, bbar, cbd, pre, pim, qre, qim = prep_s
    n, sw = us.shape
    ns = pre.shape[1]
    sds = jax.ShapeDtypeStruct
    return pl.pallas_call(
        functools.partial(_mixers_sample_kernel, seg_rows=seg_rows, start_pos=start_pos),
        out_shape=[sds((n, sw), BF16), sds(ext.shape, BF16), sds((n, 2 * ns), F32)],
        compiler_params=pltpu.CompilerParams(vmem_limit_bytes=VMEM_LIMIT),
        name="mixers_sample",
    )(us, ext, h0rep, bbar, cbd, a, pre, pim, qre, qim, l2, dskip, gluw, glub, pw_bd, pscale)


def _outproj_kernel(x_ref, at_ref, so_ref, po_ref, wo_ref, g_ref, rwh_ref, rwl_ref, rb_ref,
                    x1_ref, t_ref, gates_ref, *, n_exp, n_grp):
    mix = jnp.concatenate([at_ref[...], so_ref[...], po_ref[...]], axis=1)
    x1 = x_ref[...] + _dot(mix, wo_ref[...])
    x1_ref[...] = x1
    ms = jnp.mean(x1 * x1, axis=-1, keepdims=True)
    t = x1 * lax.rsqrt(ms + EPS) * g_ref[...]
    t_ref[...] = t.astype(BF16)
    t_hi, t_lo = _split_bf16(t)
    rwh = rwh_ref[...]
    logits = _dot(t_hi, rwh) + _dot(t_hi, rwl_ref[...]) + _dot(t_lo, rwh) + rb_ref[...]
    lane = lax.broadcasted_iota(I32, logits.shape, 1)
    big = jnp.int32(2 ** 30)
    epg = n_exp // n_grp
    is_g = (lane >= n_exp) & (lane < n_exp + n_grp)
    gl = jnp.where(is_g, logits, NEG_INF)
    gmax = jnp.max(gl, axis=1, keepdims=True)
    gidx = jnp.min(jnp.where(gl == gmax, lane - n_exp, big), axis=1, keepdims=True)
    g_w = 1.0 / jnp.sum(jnp.where(is_g, jnp.exp(gl - gmax), 0.0), axis=1, keepdims=True)
    in_grp = (lane < n_exp) & (lane // epg == gidx)
    el = jnp.where(in_grp, logits, NEG_INF)
    m1 = jnp.max(el, axis=1, keepdims=True)
    i1 = jnp.min(jnp.where(el == m1, lane, big), axis=1, keepdims=True)
    el2 = jnp.where(lane == i1, NEG_INF, el)
    m2 = jnp.max(el2, axis=1, keepdims=True)
    i2 = jnp.min(jnp.where(el2 == m2, lane, big), axis=1, keepdims=True)
    e21 = jnp.exp(m2 - m1)
    w1 = 1.0 / (1.0 + e21)
    w2 = e21 * w1
    gates_ref[...] = jnp.where(lane == i1, w1 * g_w, jnp.where(lane == i2, w2 * g_w, 0.0))


def _outproj(x2, attn, sso, poo, wo_bf, g_ffn, rw_hi, rw_lo, rb, n_exp, n_grp, tm):
    n, d = x2.shape
    aw, sw = attn.shape[1], sso.shape[1]
    row = lambda i: (i, 0)
    const = lambda i: (0, 0)
    sds = jax.ShapeDtypeStruct
    return pl.pallas_call(
        functools.partial(_outproj_kernel, n_exp=n_exp, n_grp=n_grp),
        out_shape=[sds((n, d), F32), sds((n, d), BF16), sds((n, LANES), F32)],
        grid=(n // tm,),
        in_specs=[pl.BlockSpec((tm, d), row), pl.BlockSpec((tm, aw), row), pl.BlockSpec((tm, sw), row),
                  pl.BlockSpec((tm, sw), row), pl.BlockSpec(wo_bf.shape, const), pl.BlockSpec((1, d), const),
                  pl.BlockSpec(rw_hi.shape, const), pl.BlockSpec(rw_lo.shape, const), pl.BlockSpec((1, LANES), const)],
        out_specs=[pl.BlockSpec((tm, d), row), pl.BlockSpec((tm, d), row), pl.BlockSpec((tm, LANES), row)],
        compiler_params=_cparams("parallel"),
        name="outproj",
    )(x2, attn, sso, poo, wo_bf, g_ffn, rw_hi, rw_lo, rb)


def _moe_kernel(t_ref, gates_ref, x1_ref, wg_ref, wu_ref, wd_ref, o_ref, *, epb):
    j = pl.program_id(1)

    @pl.when(j == 0)
    def _():
        o_ref[...] = x1_ref[...]

    t = t_ref[...]
    g_hi, g_lo = _split_bf16(gates_ref[...])
    g2 = jnp.concatenate([g_hi, g_lo], axis=1)
    dh = wg_ref.shape[2]
    sel_row = lax.broadcasted_iota(I32, (2 * LANES, dh), 0) % LANES
    acc = jnp.zeros(o_ref.shape, F32)
    for i in range(epb):
        e = j * epb + i
        gb = _dot(g2, (sel_row == e).astype(BF16))
        a = _dot(t, wg_ref[i])
        b = _dot(t, wu_ref[i])
        hh = (a / (1.0 + jnp.exp(-a))) * b * gb
        acc = acc + _dot(hh.astype(BF16), wd_ref[i])
    o_ref[...] += acc


def _moe(t_bf, gates, x1, wg, wu, wd, layer, tm, epb):
    n, d = x1.shape
    n_exp, _, dh = wg.shape[1:]
    row = lambda i, j: (i, 0)
    return pl.pallas_call(
        functools.partial(_moe_kernel, epb=epb),
        out_shape=jax.ShapeDtypeStruct((n, d), F32),
        grid=(n // tm, n_exp // epb),
        in_specs=[pl.BlockSpec((tm, d), row), pl.BlockSpec((tm, LANES), row), pl.BlockSpec((tm, d), row),
                  pl.BlockSpec((None, epb, d, dh), lambda i, j: (layer, j, 0, 0)),
                  pl.BlockSpec((None, epb, d, dh), lambda i, j: (layer, j, 0, 0)),
                  pl.BlockSpec((None, epb, dh, d), lambda i, j: (layer, j, 0, 0))],
        out_specs=pl.BlockSpec((tm, d), row),
        compiler_params=_cparams("parallel", "arbitrary"),
        name="moe",
    )(t_bf, gates, x1, wg, wu, wd)


def _block_diag_rows(w):
    g, a, b = w.shape
    eye = jnp.eye(g, dtype=w.dtype)
    return jnp.einsum('gab,gh->gahb', w, eye).reshape(g * a, g * b)


def _seg_cumsum_matrix(rows, seg):
    i = np.arange(rows)[:, None]
    j = np.arange(rows)[None, :]
    l = (j <= i) & (i // seg == j // seg)
    return jnp.asarray(np.concatenate([l, l], axis=1), dtype=BF16)


def kernel(x_prompt, x_sample, cache_k, cache_v, state_ssm_re, state_ssm_im, state_pool, page_table, norm_mix_g, w_in, q_norm_g, k_norm_g, sb_bias, ssm_lambda_re, ssm_lambda_im, ssm_log_dt, ssm_b_re, ssm_b_im, ssm_c_re, ssm_c_im, ssm_d, ssm_glu_w, ssm_glu_b, pool_w, pool_scale, w_out, norm_ffn_g, router_group_w, router_group_b, router_expert_w, router_expert_b, expert_w_gate, expert_w_up, expert_w_down):
    depth = w_in.shape[0]
    bsz, seq, d = x_prompt.shape
    db, dseq, _ = x_sample.shape
    n_phys, page, nh, hd = cache_k.shape[1:]
    assert hd == HEAD_DIM
    aw = nh * hd
    n_grp_ssm, n_state, grp_ch = ssm_b_re.shape[1:]
    sw = n_grp_ssm * grp_ch
    ns = n_grp_ssm * n_state
    pool_buf = state_pool.shape[2]
    assert pool_buf < POOL_HIST and max(POOL_WINDOWS) - 1 <= pool_buf
    n_exp_grp = router_group_w.shape[2]
    n_exp = router_expert_w.shape[2]
    past_len = page_table.shape[1] * page
    np_tok = bsz * seq
    ns_tok = db * dseq
    t_rows = 128

    w_in_bf = w_in.astype(BF16)
    w_out_bf = w_out.astype(BF16)
    wg_bf = expert_w_gate.astype(BF16)
    wu_bf = expert_w_up.astype(BF16)
    wd_bf = expert_w_down.astype(BF16)
    glu_w_bf = ssm_glu_w.astype(BF16)
    ones_bd = jnp.asarray(np.kron(np.eye(nh), np.ones((hd, hd))), dtype=BF16)
    cache_kt = cache_k.transpose(0, 1, 3, 4, 2).reshape(depth, n_phys, aw, page)
    cache_vt = cache_v.transpose(0, 1, 3, 4, 2).reshape(depth, n_phys, aw, page)
    head_eye = jnp.eye(nh, dtype=BF16)
    l2_prompt = _seg_cumsum_matrix(t_rows, t_rows)
    l2_sample = _seg_cumsum_matrix(ns_tok, dseq)
    seg_rows = POOL_HIST + dseq

    xp = x_prompt.reshape(np_tok, d)
    xs = x_sample.reshape(ns_tok, d)
    outs = {k: [] for k in ("kp", "vp", "hrp", "hip", "pbp", "ks", "vs", "hrs", "his", "pbs")}

    for l in range(depth):
        g_mix = norm_mix_g[l][None, :]
        qg_t = jnp.tile(q_norm_g[l], nh)[None, :]
        kg_t = jnp.tile(k_norm_g[l], nh)[None, :]
        bias_rows = jnp.broadcast_to(sb_bias[l][:, None], (nh, LANES))
        bias_rows_s = jnp.broadcast_to(jnp.repeat(sb_bias[l], dseq)[:, None], (nh * dseq, LANES))
        lr = ssm_lambda_re[l].reshape(1, ns)
        li = ssm_lambda_im[l].reshape(1, ns)
        ldt = jnp.broadcast_to(ssm_log_dt[l][:, None], (n_grp_ssm, n_state)).reshape(1, ns)
        bre_bd = _block_diag_rows(jnp.swapaxes(ssm_b_re[l], 1, 2))
        bim_bd = _block_diag_rows(jnp.swapaxes(ssm_b_im[l], 1, 2))
        cre_bd = _block_diag_rows(jnp.swapaxes(ssm_c_re[l], 1, 2))
        cim_bd = _block_diag_rows(jnp.swapaxes(ssm_c_im[l], 1, 2))
        prep = _ssm_prep(lr, li, ldt, bre_bd, bim_bd, cre_bd, cim_bd, t_rows)
        prep_s = tuple(prep[:3]) + tuple(jnp.tile(tb[:dseq], (db, 1)) for tb in prep[3:])
        dskip = ssm_d[l][None, :]
        glub = ssm_glu_b[l][None, :]
        pw_bd = _block_diag_rows(pool_w[l]).astype(BF16)
        pscale = pool_scale[l][None, :]
        g_ffn = norm_ffn_g[l][None, :]
        rw = jnp.zeros((d, LANES), F32).at[:, :n_exp].set(router_expert_w[l])
        rw = rw.at[:, n_exp:n_exp + n_exp_grp].set(router_group_w[l])
        rw_hi = rw.astype(BF16)
        rw_lo = (rw - rw_hi.astype(F32)).astype(BF16)
        rb = jnp.zeros((1, LANES), F32).at[0, :n_exp].set(router_expert_b[l])
        rb = rb.at[0, n_exp:n_exp + n_exp_grp].set(router_group_b[l])

        q_bf, k32, v32, k_bf, v_bf, us, up = _inproj(xp, g_mix, w_in_bf[l], qg_t, kg_t, ones_bd, tm=512)
        attn = _attn_prompt(q_bf, k_bf, v_bf, bias_rows, bsz, seq, tq=1024)
        sso, poo, hlast = _mixers_prompt(us, up, prep, l2_prompt, dskip, glu_w_bf[l], glub, pw_bd, pscale,
                                         bsz, seq, t_rows)
        x1, t_bf, gates = _outproj(xp, attn, sso, poo, w_out_bf[l], g_ffn, rw_hi, rw_lo, rb, n_exp, n_exp_grp, tm=512)
        xp = _moe(t_bf, gates, x1, wg_bf, wu_bf, wd_bf, l, tm=1024, epb=2)
        outs["kp"].append(k32.reshape(np_tok // page, page, nh, hd))
        outs["vp"].append(v32.reshape(np_tok // page, page, nh, hd))
        outs["hrp"].append(hlast[:, 0, :ns].reshape(bsz, n_grp_ssm, n_state))
        outs["hip"].append(hlast[:, 0, ns:].reshape(bsz, n_grp_ssm, n_state))
        outs["pbp"].append(up.reshape(bsz, seq, sw)[:, seq - pool_buf:, :])

        q_bf, k32, v32, k_bf, v_bf, us, up = _inproj(xs, g_mix, w_in_bf[l], qg_t, kg_t, ones_bd, tm=ns_tok)
        qbd = jnp.einsum('bthd,hg->bhtgd', q_bf.reshape(db, dseq, nh, hd), head_eye).reshape(db, nh * dseq, aw)
        pad_rows = ((0, 0), (0, LANES - dseq), (0, 0))
        attn = _attn_sample(qbd, jnp.pad(k32.reshape(db, dseq, aw), pad_rows), jnp.pad(v32.reshape(db, dseq, aw), pad_rows),
                            cache_kt, cache_vt, page_table, bias_rows_s, l, n_pages_step=16, nq=dseq)
        ext = jnp.concatenate([jnp.zeros((db, POOL_HIST - pool_buf, sw), F32), state_pool[l],
                               up.reshape(db, dseq, sw)], axis=1)
        h0 = jnp.concatenate([state_ssm_re[l].reshape(db, ns), state_ssm_im[l].reshape(db, ns)], axis=1)
        h0rep = jnp.repeat(h0, dseq, axis=0)
        sso, poo_ext, hall = _mixers_sample(us, ext.reshape(db * seg_rows, sw), h0rep, prep_s, l2_sample, dskip,
                                            glu_w_bf[l], glub, pw_bd, pscale, seg_rows, past_len)
        poo = poo_ext.reshape(db, seg_rows, sw)[:, POOL_HIST:, :].reshape(ns_tok, sw)
        x1, t_bf, gates = _outproj(xs, attn.reshape(ns_tok, aw), sso, poo, w_out_bf[l], g_ffn, rw_hi, rw_lo, rb,
                                   n_exp, n_exp_grp, tm=ns_tok)
        xs = _moe(t_bf, gates, x1, wg_bf, wu_bf, wd_bf, l, tm=ns_tok, epb=2)
        hl = hall.reshape(db, dseq, 2 * ns)[:, dseq - 1, :]
        outs["ks"].append(k32.reshape(db, dseq, nh, hd))
        outs["vs"].append(v32.reshape(db, dseq, nh, hd))
        outs["hrs"].append(hl[:, :ns].reshape(db, n_grp_ssm, n_state))
        outs["his"].append(hl[:, ns:].reshape(db, n_grp_ssm, n_state))
        outs["pbs"].append(ext[:, seg_rows - pool_buf:, :])

    st = {k: jnp.stack(v) for k, v in outs.items()}
    return (xp.reshape(bsz, seq, d), xs.reshape(db, dseq, d), st["kp"], st["vp"], st["hrp"], st["hip"], st["pbp"],
            st["ks"], st["vs"], st["hrs"], st["his"], st["pbs"])
```

```python
import functools
import math

import numpy as np
import jax
import jax.numpy as jnp
from jax import lax
from jax.experimental import pallas as pl
from jax.experimental.pallas import tpu as pltpu

F32 = jnp.float32
BF16 = jnp.bfloat16
I32 = jnp.int32

EPS = 1e-6
HEAD_DIM = 64
POOL_WINDOWS = (2, 4, 8, 16)
POOL_HIST = 16
TOP_K_INNER = 2
LANES = 128
VMEM_LIMIT = 56 * 1024 * 1024

NEG_INF = float("-inf")
LOG2E = math.log2(math.e)
ROUTE_LANE = 64
MOE_ROW_ALIGN = 16
MOE_CHUNK = 128


def _cparams(*sem):
    return pltpu.CompilerParams(dimension_semantics=sem, vmem_limit_bytes=VMEM_LIMIT)


def _split_bf16(x):
    hi = x.astype(BF16)
    lo = (x - hi.astype(F32)).astype(BF16)
    return hi, lo


def _dot(a, b):
    return jnp.dot(a, b, preferred_element_type=F32)


def _dot_nt(a, b):
    return lax.dot_general(a, b, (((1,), (1,)), ((), ())), preferred_element_type=F32)


def _dot_tn(a, b):
    return lax.dot_general(a, b, (((0,), (0,)), ((), ())), preferred_element_type=F32)


def _neg_abs(z):
    bits = lax.bitcast_convert_type(z, jnp.uint32) | jnp.uint32(0x80000000)
    return lax.bitcast_convert_type(bits, F32)


def _neg_softplus2(z):
    return -(jnp.maximum(z, 0.0) + jnp.log2(1.0 + jnp.exp2(_neg_abs(z))))


def _inproj_kernel(x_ref, g_ref, w_ref, qg_ref, kg_ref, ones_ref,
                   q_ref, k_ref, v_ref, kb_ref, vb_ref, us_ref, up_ref, *, aw, sw):
    x = x_ref[...]
    ms = jnp.mean(x * x, axis=-1, keepdims=True)
    h = (x * lax.rsqrt(ms + EPS) * g_ref[...]).astype(BF16)
    proj = _dot(h, w_ref[...])
    ones_bd = ones_ref[...]

    def headnorm(t, gain):
        hi, lo = _split_bf16(t * t)
        ss = _dot(hi, ones_bd) + _dot(lo, ones_bd)
        return t * lax.rsqrt(ss * (1.0 / HEAD_DIM) + EPS) * gain

    q = headnorm(proj[:, :aw], qg_ref[...])
    k = headnorm(proj[:, aw:2 * aw], kg_ref[...])
    v = proj[:, 2 * aw:3 * aw]
    q_ref[...] = (q * (LOG2E * HEAD_DIM ** -0.5)).astype(BF16)
    k_ref[...] = k
    v_ref[...] = v
    kb_ref[...] = k.astype(BF16)
    vb_ref[...] = v.astype(BF16)
    us_ref[...] = proj[:, 3 * aw:3 * aw + sw]
    up_ref[...] = proj[:, 3 * aw + sw:]


def _inproj(x2, g_mix, w_in_bf, qg_t, kg_t, ones_bd, tm):
    n, d = x2.shape
    aw = ones_bd.shape[0]
    sw = (w_in_bf.shape[1] - 3 * aw) // 2
    row = lambda i: (i, 0)
    const = lambda i: (0, 0)
    sds = jax.ShapeDtypeStruct
    return pl.pallas_call(
        functools.partial(_inproj_kernel, aw=aw, sw=sw),
        out_shape=[sds((n, aw), BF16), sds((n, aw), F32), sds((n, aw), F32),
                   sds((n, aw), BF16), sds((n, aw), BF16), sds((n, sw), F32), sds((n, sw), F32)],
        grid=(n // tm,),
        in_specs=[pl.BlockSpec((tm, d), row), pl.BlockSpec((1, d), const),
                  pl.BlockSpec(w_in_bf.shape, const), pl.BlockSpec((1, aw), const),
                  pl.BlockSpec((1, aw), const), pl.BlockSpec((aw, aw), const)],
        out_specs=[pl.BlockSpec((tm, aw), row)] * 5 + [pl.BlockSpec((tm, sw), row)] * 2,
        compiler_params=_cparams("parallel"),
        name="inproj",
    )(x2, g_mix, w_in_bf, qg_t, kg_t, ones_bd)


def _attn_prompt_kernel(qi_ref, kj_ref, q_ref, k_ref, v_ref, uu_ref, bias_ref, o_ref,
                        r_ref, acc_ref, *, tq, tkb, sub, rc):
    s = pl.program_id(2)
    hp = pl.program_id(1)
    qi = qi_ref[s]
    kj = kj_ref[s]
    is_diag = kj == qi

    @pl.when(is_diag)
    def _():
        r_ref[...] = jnp.zeros_like(r_ref)
        acc_ref[...] = jnp.zeros_like(acc_ref)

    n_sub = tkb // sub
    n_rc = tq // rc
    lane_k = lax.broadcasted_iota(I32, (sub, LANES), 1)
    d_idx = lax.broadcasted_iota(I32, (rc, sub), 1) - lax.broadcasted_iota(I32, (rc, sub), 0)
    biases = (bias_ref[pl.ds(2 * hp, 1), :] * LOG2E, bias_ref[pl.ds(2 * hp + 1, 1), :] * LOG2E)
    uu = uu_ref[...]
    zero_bf = jnp.zeros((), BF16)
    head_lanes = (lane_k < HEAD_DIM, lane_k >= HEAD_DIM)

    def run(tiles, masked):
        kv = {}

        def kv_cat(c):
            if c not in kv:
                kc = k_ref[pl.ds(c * sub, sub), :]
                vc = v_ref[pl.ds(c * sub, sub), :]
                kv[c] = (jnp.concatenate([jnp.where(m, kc, zero_bf) for m in head_lanes], axis=0),
                         jnp.concatenate([jnp.where(m, vc, zero_bf) for m in head_lanes], axis=0))
            return kv[c]

        def scores(c, r):
            return _dot_nt(q_ref[pl.ds(r * rc, rc), :], kv_cat(c)[0])

        def suffix_sums(c, r, z2):
            valid = d_idx < (r * rc - c * sub) if masked else None
            zs, cums = [], []
            for hd in range(2):
                z = z2[:, hd * sub:(hd + 1) * sub] + biases[hd]
                sp = jnp.maximum(z, 0.0) + jnp.log2(1.0 + jnp.exp2(_neg_abs(z)))
                if masked:
                    sp = jnp.where(valid, sp, 0.0)
                zs.append(z)
                cums.append(_dot(sp.astype(BF16), uu[:sub]))
            return zs, cums, valid

        def weights_and_values(c, r, zs, cums, valid):
            rows = pl.ds(r * rc, rc)
            ps = []
            for hd in range(2):
                rr = r_ref[hd, rows, :]
                p = jnp.exp2(zs[hd] - (cums[hd] + rr))
                if masked:
                    p = jnp.where(valid, p, 0.0)
                r_ref[hd, rows, :] = rr + jnp.broadcast_to(cums[hd][:, 0:1], rr.shape)
                ps.append(p.astype(BF16))
            acc_ref[rows, :] += _dot(jnp.concatenate(ps, axis=1), kv_cat(c)[1])

        z2 = scores(*tiles[0])
        for n, (c, r) in enumerate(tiles):
            mid = suffix_sums(c, r, z2)
            if n + 1 < len(tiles):
                z2 = scores(*tiles[n + 1])
            weights_and_values(c, r, *mid)

    @pl.when(is_diag)
    def _():
        run([(c, r) for c in range(n_sub - 1, -1, -1) for r in range((c * sub) // rc, n_rc)], True)

    @pl.when(jnp.logical_not(is_diag))
    def _():
        run([(c, r) for c in range(n_sub - 1, -1, -1) for r in range(n_rc)], False)

    @pl.when(kj == 0)
    def _():
        o_ref[...] = acc_ref[...].astype(o_ref.dtype)


def _suffix_matrix(sub):
    j = np.arange(2 * sub)[:, None] % sub
    s = np.arange(sub)[None, :]
    return jnp.asarray(j >= s, dtype=BF16)


def _attn_prompt(q_bf, k_bf, v_bf, bias_rows, bsz, seq, tq):
    n, aw = q_bf.shape
    tkb = tq
    sub = LANES
    rc = 512
    nq = seq // tq
    qi_tbl = np.concatenate([np.full(i + 1, i) for i in range(nq)]).astype(np.int32)
    kj_tbl = np.concatenate([np.arange(i, -1, -1) for i in range(nq)]).astype(np.int32)
    n_steps = qi_tbl.shape[0]
    n_hp = aw // LANES
    uu = _suffix_matrix(sub)
    grid_spec = pltpu.PrefetchScalarGridSpec(
        num_scalar_prefetch=2,
        grid=(bsz, n_hp, n_steps),
        in_specs=[
            pl.BlockSpec((tq, LANES), lambda b, h, s, qi, kj: (b * nq + qi[s], h)),
            pl.BlockSpec((tkb, LANES), lambda b, h, s, qi, kj: (b * nq + kj[s], h)),
            pl.BlockSpec((tkb, LANES), lambda b, h, s, qi, kj: (b * nq + kj[s], h)),
            pl.BlockSpec(uu.shape, lambda b, h, s, qi, kj: (0, 0)),
            pl.BlockSpec(bias_rows.shape, lambda b, h, s, qi, kj: (0, 0)),
        ],
        out_specs=pl.BlockSpec((tq, LANES), lambda b, h, s, qi, kj: (b * nq + qi[s], h)),
        scratch_shapes=[pltpu.VMEM((2, tq, LANES), F32), pltpu.VMEM((tq, LANES), F32)],
    )
    return pl.pallas_call(
        functools.partial(_attn_prompt_kernel, tq=tq, tkb=tkb, sub=sub, rc=rc),
        out_shape=jax.ShapeDtypeStruct((n, aw), BF16),
        grid_spec=grid_spec,
        compiler_params=_cparams("parallel", "parallel", "arbitrary"),
        name="attn_prompt",
    )(jnp.asarray(qi_tbl), jnp.asarray(kj_tbl), q_bf, k_bf, v_bf, uu, bias_rows)


def _attn_sample_kernel(pt_ref, q_ref, kn_ref, vn_ref, bias_ref, uu_ref, *rest, n_pages_step, nq, nh):
    k_refs = rest[:n_pages_step]
    v_refs = rest[n_pages_step:2 * n_pages_step]
    o_ref = rest[2 * n_pages_step]
    r_ref, acc_ref = rest[2 * n_pages_step + 1:]
    del pt_ref
    j = pl.program_id(1)
    nrow = nh * nq
    bias = bias_ref[...] * LOG2E
    qbd = q_ref[...]
    uu = uu_ref[...]

    def softplus2(z):
        return jnp.maximum(z, 0.0) + jnp.log2(1.0 + jnp.exp2(_neg_abs(z)))

    @pl.when(j == 0)
    def _():
        z = _dot_nt(qbd, kn_ref[...].astype(BF16)) + bias
        lane_tok = lax.broadcasted_iota(I32, (nrow, LANES), 1)
        row_q = lax.broadcasted_iota(I32, (nrow, LANES), 0) % nq
        valid = lane_tok < row_q
        hi, lo = _split_bf16(jnp.where(valid, softplus2(z), 0.0))
        ct = _dot(jnp.concatenate([hi, lo], axis=1), uu)
        p = jnp.where(valid, jnp.exp2(z - ct[:, :LANES]), 0.0)
        r_ref[...] = ct[:, LANES:]
        acc_ref[...] = _dot(p.astype(BF16), vn_ref[...].astype(BF16))

    pages = range(n_pages_step - 1, -1, -1)
    zz = {i: _dot(qbd, k_refs[i][...].astype(BF16)) + bias for i in pages}
    hi, lo = _split_bf16(jnp.concatenate([softplus2(zz[i]) for i in pages], axis=0))
    ct = _dot(jnp.concatenate([hi, lo], axis=1), uu)
    rr = r_ref[...]
    ps = {}
    for n, i in enumerate(pages):
        ps[i] = jnp.exp2(zz[i] - (ct[n * nrow:(n + 1) * nrow, :LANES] + rr)).astype(BF16)
        rr = rr + ct[n * nrow:(n + 1) * nrow, LANES:]
    acc = acc_ref[...]
    for i in pages:
        acc = acc + _dot_nt(ps[i], v_refs[i][...].astype(BF16))
    r_ref[...] = rr
    acc_ref[...] = acc

    @pl.when(j == pl.num_programs(1) - 1)
    def _():
        row_head = lax.broadcasted_iota(I32, acc.shape, 0) // nq
        col_head = lax.broadcasted_iota(I32, acc.shape, 1) // HEAD_DIM
        own = jnp.where(row_head == col_head, acc, 0.0)
        out = own[0:nq, :]
        for h in range(1, nh):
            out = out + own[h * nq:(h + 1) * nq, :]
        o_ref[...] = out.astype(o_ref.dtype)


def _attn_sample(qbd, kn_pad, vn_pad, cache_kt, cache_vt, page_table, bias_rows, layer, n_pages_step, nq):
    db, nrow, aw = qbd.shape
    nh = nrow // nq
    page = cache_kt.shape[3]
    assert page == LANES
    n_pages = page_table.shape[1]
    n_steps = n_pages // n_pages_step
    jj = np.arange(2 * LANES)[:, None] % LANES
    ss = np.arange(2 * LANES)[None, :]
    uu = jnp.asarray((ss >= LANES) | (jj >= ss), dtype=BF16)

    def page_map(i):
        return lambda b, j, pt: (layer, pt[b, (n_steps - 1 - j) * n_pages_step + i], 0, 0)

    seq_map = lambda b, j, pt: (b, 0, 0)
    const = lambda b, j, pt: (0, 0)
    page_spec = [pl.BlockSpec((None, None, aw, page), page_map(i)) for i in range(n_pages_step)]
    grid_spec = pltpu.PrefetchScalarGridSpec(
        num_scalar_prefetch=1,
        grid=(db, n_steps),
        in_specs=[pl.BlockSpec((None, nrow, aw), seq_map), pl.BlockSpec((None, LANES, aw), seq_map),
                  pl.BlockSpec((None, LANES, aw), seq_map), pl.BlockSpec(bias_rows.shape, const),
                  pl.BlockSpec(uu.shape, const)] + page_spec + page_spec,
        out_specs=pl.BlockSpec((None, nq, aw), seq_map),
        scratch_shapes=[pltpu.VMEM((nrow, LANES), F32), pltpu.VMEM((nrow, aw), F32)],
    )
    return pl.pallas_call(
        functools.partial(_attn_sample_kernel, n_pages_step=n_pages_step, nq=nq, nh=nh),
        out_shape=jax.ShapeDtypeStruct((db, nq, aw), BF16),
        grid_spec=grid_spec,
        compiler_params=_cparams("parallel", "arbitrary"),
        name="attn_sample",
    )(page_table, qbd, kn_pad, vn_pad, bias_rows, uu, *([cache_kt] * n_pages_step), *([cache_vt] * n_pages_step))


def _ssm_prep_kernel(lr_ref, li_ref, ldt_ref, bre_ref, bim_ref, cre_ref, cim_ref,
                     a_ref, bbar_ref, cbd_ref, pre_ref, pim_ref, qre_ref, qim_ref, *, t_rows):
    lr = lr_ref[...]
    li = li_ref[...]
    dt = jnp.exp(ldt_ref[...])
    mag = jnp.exp(lr * dt)
    ar = mag * jnp.cos(li * dt)
    ai = mag * jnp.sin(li * dt)
    er = ar - 1.0
    den = lr * lr + li * li
    fr = (er * lr + ai * li) / den
    fi = (ai * lr - er * li) / den
    a_ref[0:1, :] = ar
    a_ref[1:2, :] = ai
    bre = bre_ref[...]
    bim = bim_ref[...]
    ns = lr.shape[1]
    bbar_ref[:, :ns] = (fr * bre - fi * bim).astype(BF16)
    bbar_ref[:, ns:] = (fr * bim + fi * bre).astype(BF16)
    cbd_ref[:ns, :] = cre_ref[...].astype(BF16)
    cbd_ref[ns:, :] = (-cim_ref[...]).astype(BF16)
    t = lax.broadcasted_iota(I32, (t_rows, ns), 0).astype(F32)
    grow = jnp.exp(t * (lr * dt))
    shrink = jnp.exp(-t * (lr * dt))
    ang = t * (li * dt)
    c = jnp.cos(ang)
    s = jnp.sin(ang)
    pre_ref[...] = grow * c
    pim_ref[...] = grow * s
    qre_ref[...] = shrink * c
    qim_ref[...] = -(shrink * s)


def _ssm_prep(lr, li, ldt, bre_bd, bim_bd, cre_bd, cim_bd, t_rows):
    ns = lr.shape[1]
    sw = bre_bd.shape[0]
    sds = jax.ShapeDtypeStruct
    return pl.pallas_call(
        functools.partial(_ssm_prep_kernel, t_rows=t_rows),
        out_shape=[sds((2, ns), F32), sds((sw, 2 * ns), BF16), sds((2 * ns, sw), BF16)]
        + [sds((t_rows, ns), F32)] * 4,
        compiler_params=pltpu.CompilerParams(vmem_limit_bytes=VMEM_LIMIT),
        name="ssm_prep",
    )(lr, li, ldt, bre_bd, bim_bd, cre_bd, cim_bd)


def _ssm_compute(u32, bbar, cbd, a, pre, pim, qre, qim, l2, hp_re, hp_im, dskip, gluw, glub):
    ns = pre.shape[1]
    bu = _dot(u32.astype(BF16), bbar)
    bur, bui = bu[:, :ns], bu[:, ns:]
    xr = qre * bur - qim * bui
    xi = qre * bui + qim * bur
    x_hi, x_lo = _split_bf16(jnp.concatenate([xr, xi], axis=1))
    s = _dot(l2, jnp.concatenate([x_hi, x_lo], axis=0))
    ar, ai = a[0:1, :], a[1:2, :]
    sr = s[:, :ns] + (ar * hp_re - ai * hp_im)
    si = s[:, ns:] + (ar * hp_im + ai * hp_re)
    hr = pre * sr - pim * si
    hi_ = pre * si + pim * sr
    hcat = jnp.concatenate([hr, hi_], axis=1)
    y = _dot(hcat.astype(BF16), cbd) + dskip * u32
    g = 0.5 * y * (1.0 + jnp.tanh(math.sqrt(2.0 / math.pi) * (y + 0.044715 * (y * y * y))))
    gate = 1.0 / (1.0 + jnp.exp(-(_dot(g.astype(BF16), gluw) + glub)))
    return g * gate, hcat


def _pool_compute(ext, pos, pw_bd, pscale):
    s2 = ext + pltpu.roll(ext, 1, 0)
    s4 = s2 + pltpu.roll(s2, 2, 0)
    s8 = s4 + pltpu.roll(s4, 4, 0)
    s16 = s8 + pltpu.roll(s8, 8, 0)
    grp = lax.broadcasted_iota(I32, ext.shape, 1) // (ext.shape[1] // len(POOL_WINDOWS))
    win = jnp.where(grp == 0, s2, jnp.where(grp == 1, s4, jnp.where(grp == 2, s8, s16)))
    w = jnp.where(grp == 0, POOL_WINDOWS[0],
                  jnp.where(grp == 1, POOL_WINDOWS[1], jnp.where(grp == 2, POOL_WINDOWS[2], POOL_WINDOWS[3])))
    cnt = jnp.maximum(jnp.minimum(pos + 1, w), 1).astype(F32)
    d = win / cnt - ext
    return _dot(d.astype(BF16), pw_bd) * pscale


def _mixers_prompt_kernel(us_ref, up_ref, bbar_ref, cbd_ref, a_ref, pre_ref, pim_ref, qre_ref, qim_ref, l2_ref,
                          dskip_ref, gluw_ref, glub_ref, pw_ref, ps_ref,
                          so_ref, po_ref, hl_ref, carry_ref, ext_ref, *, t_rows):
    c = pl.program_id(1)
    ns = pre_ref.shape[1]

    @pl.when(c == 0)
    def _():
        carry_ref[...] = jnp.zeros_like(carry_ref)
        ext_ref[0:POOL_HIST, :] = jnp.zeros((POOL_HIST, ext_ref.shape[1]), F32)

    carry = carry_ref[...]
    out, hcat = _ssm_compute(us_ref[...], bbar_ref[...], cbd_ref[...], a_ref[...], pre_ref[...], pim_ref[...],
                             qre_ref[...], qim_ref[...], l2_ref[...], carry[:, :ns], carry[:, ns:],
                             dskip_ref[...], gluw_ref[...], glub_ref[...])
    so_ref[...] = out.astype(so_ref.dtype)
    last = hcat[t_rows - 1:t_rows, :]
    carry_ref[...] = last
    hl_ref[...] = last

    ext_ref[POOL_HIST:, :] = up_ref[...]
    ext = ext_ref[...]
    pos = c * t_rows - POOL_HIST + lax.broadcasted_iota(I32, (t_rows + POOL_HIST, 1), 0)
    pooled = _pool_compute(ext, pos, pw_ref[...], ps_ref[...])
    po_ref[...] = pooled[POOL_HIST:, :].astype(po_ref.dtype)
    ext_ref[0:POOL_HIST, :] = ext[t_rows:, :]


def _mixers_prompt(us, up, prep, l2, dskip, gluw, glub, pw_bd, pscale, bsz, seq, t_rows):
    a, bbar, cbd, pre, pim, qre, qim = prep
    n, sw = us.shape
    ns = pre.shape[1]
    nc = seq // t_rows
    row = lambda b, c: (b * nc + c, 0)
    const = lambda b, c: (0, 0)
    full = lambda arr: pl.BlockSpec(arr.shape, const)
    sds = jax.ShapeDtypeStruct
    return pl.pallas_call(
        functools.partial(_mixers_prompt_kernel, t_rows=t_rows),
        out_shape=[sds((n, sw), BF16), sds((n, sw), BF16), sds((bsz, 1, 2 * ns), F32)],
        grid=(bsz, nc),
        in_specs=[pl.BlockSpec((t_rows, sw), row), pl.BlockSpec((t_rows, sw), row),
                  full(bbar), full(cbd), full(a), full(pre), full(pim), full(qre), full(qim), full(l2),
                  full(dskip), full(gluw), full(glub), full(pw_bd), full(pscale)],
        out_specs=[pl.BlockSpec((t_rows, sw), row), pl.BlockSpec((t_rows, sw), row),
                   pl.BlockSpec((None, 1, 2 * ns), lambda b, c: (b, 0, 0))],
        scratch_shapes=[pltpu.VMEM((1, 2 * ns), F32), pltpu.VMEM((t_rows + POOL_HIST, sw), F32)],
        compiler_params=_cparams("parallel", "arbitrary"),
        name="mixers_prompt",
    )(us, up, bbar, cbd, a, pre, pim, qre, qim, l2, dskip, gluw, glub, pw_bd, pscale)


def _mixers_sample_kernel(us_ref, ext_ref, h0_ref, bbar_ref, cbd_ref, a_ref, pre_ref, pim_ref, qre_ref, qim_ref,
                          l2_ref, dskip_ref, gluw_ref, glub_ref, pw_ref, ps_ref,
                          so_ref, po_ref, h_ref, *, seg_rows, start_pos):
    ns = pre_ref.shape[1]
    h0 = h0_ref[...]
    out, hcat = _ssm_compute(us_ref[...], bbar_ref[...], cbd_ref[...], a_ref[...], pre_ref[...], pim_ref[...],
                             qre_ref[...], qim_ref[...], l2_ref[...], h0[:, :ns], h0[:, ns:],
                             dskip_ref[...], gluw_ref[...], glub_ref[...])
    so_ref[...] = out.astype(so_ref.dtype)
    h_ref[...] = hcat
    ext = ext_ref[...]
    pos = start_pos - POOL_HIST + lax.broadcasted_iota(I32, (ext.shape[0], 1), 0) % seg_rows
    po_ref[...] = _pool_compute(ext, pos, pw_ref[...], ps_ref[...]).astype(po_ref.dtype)


def _mixers_sample(us, ext, h0rep, prep_s, l2, dskip, gluw, glub, pw_bd, pscale, seg_rows, start_pos):
    a, bbar, cbd, pre, pim, qre, qim = prep_s
    n, sw = us.shape
    ns = pre.shape[1]
    sds = jax.ShapeDtypeStruct
    return pl.pallas_call(
        functools.partial(_mixers_sample_kernel, seg_rows=seg_rows, start_pos=start_pos),
        out_shape=[sds((n, sw), BF16), sds(ext.shape, BF16), sds((n, 2 * ns), F32)],
        compiler_params=pltpu.CompilerParams(vmem_limit_bytes=VMEM_LIMIT),
        name="mixers_sample",
    )(us, ext, h0rep, bbar, cbd, a, pre, pim, qre, qim, l2, dskip, gluw, glub, pw_bd, pscale)


def _outproj_kernel(x_ref, at_ref, so_ref, po_ref, wo_ref, g_ref, rwh_ref, rwl_ref, rb_ref,
                    x1_ref, t_ref, gates_ref, *, n_exp, n_grp):
    mix = jnp.concatenate([at_ref[...], so_ref[...], po_ref[...]], axis=1)
    x1 = x_ref[...] + _dot(mix, wo_ref[...])
    x1_ref[...] = x1
    ms = jnp.mean(x1 * x1, axis=-1, keepdims=True)
    t = x1 * lax.rsqrt(ms + EPS) * g_ref[...]
    t_ref[...] = t.astype(BF16)
    t_hi, t_lo = _split_bf16(t)
    rwh = rwh_ref[...]
    logits = _dot(t_hi, rwh) + _dot(t_hi, rwl_ref[...]) + _dot(t_lo, rwh) + rb_ref[...]
    lane = lax.broadcasted_iota(I32, logits.shape, 1)
    big = jnp.int32(2 ** 30)
    epg = n_exp // n_grp
    is_g = (lane >= n_exp) & (lane < n_exp + n_grp)
    gl = jnp.where(is_g, logits, NEG_INF)
    gmax = jnp.max(gl, axis=1, keepdims=True)
    gidx = jnp.min(jnp.where(gl == gmax, lane - n_exp, big), axis=1, keepdims=True)
    g_w = 1.0 / jnp.sum(jnp.where(is_g, jnp.exp(gl - gmax), 0.0), axis=1, keepdims=True)
    in_grp = (lane < n_exp) & (lane // epg == gidx)
    el = jnp.where(in_grp, logits, NEG_INF)
    m1 = jnp.max(el, axis=1, keepdims=True)
    i1 = jnp.min(jnp.where(el == m1, lane, big), axis=1, keepdims=True)
    el2 = jnp.where(lane == i1, NEG_INF, el)
    m2 = jnp.max(el2, axis=1, keepdims=True)
    i2 = jnp.min(jnp.where(el2 == m2, lane, big), axis=1, keepdims=True)
    e21 = jnp.exp(m2 - m1)
    w1 = 1.0 / (1.0 + e21)
    w2 = e21 * w1
    gates = jnp.where(lane == i1, w1 * g_w, jnp.where(lane == i2, w2 * g_w, 0.0))
    route = jnp.where(lane == ROUTE_LANE, i1.astype(F32),
                      jnp.where(lane == ROUTE_LANE + 1, i2.astype(F32),
                                jnp.where(lane == ROUTE_LANE + 2, w1 * g_w,
                                          jnp.where(lane == ROUTE_LANE + 3, w2 * g_w, gates))))
    gates_ref[...] = route


def _outproj(x2, attn, sso, poo, wo_bf, g_ffn, rw_hi, rw_lo, rb, n_exp, n_grp, tm):
    n, d = x2.shape
    aw, sw = attn.shape[1], sso.shape[1]
    row = lambda i: (i, 0)
    const = lambda i: (0, 0)
    sds = jax.ShapeDtypeStruct
    return pl.pallas_call(
        functools.partial(_outproj_kernel, n_exp=n_exp, n_grp=n_grp),
        out_shape=[sds((n, d), F32), sds((n, d), BF16), sds((n, LANES), F32)],
        grid=(n // tm,),
        in_specs=[pl.BlockSpec((tm, d), row), pl.BlockSpec((tm, aw), row), pl.BlockSpec((tm, sw), row),
                  pl.BlockSpec((tm, sw), row), pl.BlockSpec(wo_bf.shape, const), pl.BlockSpec((1, d), const),
                  pl.BlockSpec(rw_hi.shape, const), pl.BlockSpec(rw_lo.shape, const), pl.BlockSpec((1, LANES), const)],
        out_specs=[pl.BlockSpec((tm, d), row), pl.BlockSpec((tm, d), row), pl.BlockSpec((tm, LANES), row)],
        compiler_params=_cparams("parallel"),
        name="outproj",
    )(x2, attn, sso, poo, wo_bf, g_ffn, rw_hi, rw_lo, rb)


def _moe_kernel(t_ref, gates_ref, x1_ref, wg_ref, wu_ref, wd_ref, o_ref, *, epb):
    j = pl.program_id(1)

    @pl.when(j == 0)
    def _():
        o_ref[...] = x1_ref[...]

    t = t_ref[...]
    g_hi, g_lo = _split_bf16(gates_ref[...])
    g2 = jnp.concatenate([g_hi, g_lo], axis=1)
    dh = wg_ref.shape[2]
    sel_row = lax.broadcasted_iota(I32, (2 * LANES, dh), 0) % LANES
    acc = jnp.zeros(o_ref.shape, F32)
    for i in range(epb):
        e = j * epb + i
        gb = _dot(g2, (sel_row == e).astype(BF16))
        a = _dot(t, wg_ref[i])
        b = _dot(t, wu_ref[i])
        hh = (a / (1.0 + jnp.exp(-a))) * b * gb
        acc = acc + _dot(hh.astype(BF16), wd_ref[i])
    o_ref[...] += acc


def _moe(t_bf, gates, x1, wg, wu, wd, layer, tm, epb):
    n, d = x1.shape
    n_exp, _, dh = wg.shape[1:]
    row = lambda i, j: (i, 0)
    return pl.pallas_call(
        functools.partial(_moe_kernel, epb=epb),
        out_shape=jax.ShapeDtypeStruct((n, d), F32),
        grid=(n // tm, n_exp // epb),
        in_specs=[pl.BlockSpec((tm, d), row), pl.BlockSpec((tm, LANES), row), pl.BlockSpec((tm, d), row),
                  pl.BlockSpec((None, epb, d, dh), lambda i, j: (layer, j, 0, 0)),
                  pl.BlockSpec((None, epb, d, dh), lambda i, j: (layer, j, 0, 0)),
                  pl.BlockSpec((None, epb, dh, d), lambda i, j: (layer, j, 0, 0))],
        out_specs=pl.BlockSpec((tm, d), row),
        compiler_params=_cparams("parallel", "arbitrary"),
        name="moe",
    )(t_bf, gates, x1, wg, wu, wd)


def _moe_routed_kernel(off_ref, end_ref, nch_ref, t_ref, x1_ref, dt_ref, wt_ref, dc_ref, wg_ref, wu_ref, wd_ref, o_ref,
                       xs_ref, ys_ref, gs_ref, *, epb, n_exp, slots):
    i = pl.program_id(0)
    j = pl.program_id(1)
    tm = t_ref.shape[0]

    @pl.when(j == 0)
    def _():
        t = t_ref[...]
        d1, d2 = dt_ref[0:1, :], dt_ref[1:2, :]
        w1, w2 = wt_ref[0:1, :], wt_ref[1:2, :]
        for c in range(slots // MOE_CHUNK):
            rows = slice(c * MOE_CHUNK, (c + 1) * MOE_CHUNK)
            slot = lax.broadcasted_iota(I32, (MOE_CHUNK, tm), 0) + c * MOE_CHUNK
            m1, m2 = slot == d1, slot == d2
            sel = jnp.where(m1 | m2, 1.0, 0.0).astype(BF16)
            xs_ref[rows, :] = _dot(sel, t).astype(BF16)
            g = jnp.sum(jnp.where(m1, w1, 0.0) + jnp.where(m2, w2, 0.0), axis=1, keepdims=True)
            gs_ref[rows, :] = jnp.broadcast_to(g, (MOE_CHUNK, LANES))
        xs_ref[slots:, :] = jnp.zeros((xs_ref.shape[0] - slots, xs_ref.shape[1]), BF16)
        gs_ref[slots:, :] = jnp.zeros((gs_ref.shape[0] - slots, LANES), F32)
        ys_ref[...] = jnp.zeros_like(ys_ref)

    def expert_chunk(k, start, end=None):
        rows = pl.ds(pl.multiple_of(start, MOE_ROW_ALIGN), MOE_CHUNK)
        x = xs_ref[rows, :]
        a = _dot(x, wg_ref[k])
        b = _dot(x, wu_ref[k])
        g = gs_ref[rows, :]
        hh = (a / (1.0 + jnp.exp(-a))) * b * jnp.concatenate([g] * (a.shape[1] // LANES), axis=1)
        y = _dot(hh.astype(BF16), wd_ref[k])
        if end is not None:
            slot = start + lax.broadcasted_iota(I32, y.shape, 0)
            y = jnp.where(slot < end, y, ys_ref[rows, :].astype(F32))
        ys_ref[rows, :] = y.astype(BF16)

    offs = [off_ref[i * n_exp + j * epb + k] for k in range(epb)]
    for k in range(epb):
        expert_chunk(k, offs[k])
    for k in range(epb):
        n_chunks = nch_ref[i * n_exp + j * epb + k]
        end = end_ref[i * n_exp + j * epb + k]

        def more(c, carry, k=k, end=end):
            expert_chunk(k, offs[k] + c * MOE_CHUNK, end)
            return carry

        lax.fori_loop(1, n_chunks, more, 0)

    @pl.when(j == pl.num_programs(1) - 1)
    def _():
        dc1, dc2 = dc_ref[:, 0:1], dc_ref[:, 1:2]
        acc = x1_ref[...]
        kw = 2 * LANES
        for kc in range(slots // kw):
            slot = lax.broadcasted_iota(I32, (tm, kw), 1) + kc * kw
            st = jnp.where((slot == dc1) | (slot == dc2), 1.0, 0.0).astype(BF16)
            acc = acc + _dot(st, ys_ref[kc * kw:(kc + 1) * kw, :])
        o_ref[...] = acc


def _moe_routed(t_bf, route, x1, wg, wu, wd, layer, tm, epb):
    n, d = x1.shape
    n_exp, _, dh = wg.shape[1:]
    n_tiles = n // tm
    slots = -(-(tm * TOP_K_INNER + n_exp * (MOE_ROW_ALIGN - 1)) // (2 * LANES)) * (2 * LANES)
    e12 = route[:, ROUTE_LANE:ROUTE_LANE + TOP_K_INNER].astype(I32).reshape(n_tiles, tm, TOP_K_INNER)
    w12 = route[:, ROUTE_LANE + TOP_K_INNER:ROUTE_LANE + 2 * TOP_K_INNER].reshape(n_tiles, tm, TOP_K_INNER)
    picked = jnp.sum(jax.nn.one_hot(e12, n_exp, dtype=I32), axis=2)
    cum = jnp.cumsum(picked, axis=1)
    cnt = cum[:, -1, :]
    cnt_al = -(-cnt // MOE_ROW_ALIGN) * MOE_ROW_ALIGN
    off = jnp.cumsum(cnt_al, axis=1) - cnt_al
    dest = jnp.take_along_axis(off[:, None, :] + cum - 1, e12, axis=2)
    n_chunks = -(-cnt // MOE_CHUNK)
    pad8 = ((0, 0), (0, 8 - TOP_K_INNER), (0, 0))
    dest_t = jnp.pad(dest.transpose(0, 2, 1), pad8, constant_values=-1)
    w_t = jnp.pad(w12.transpose(0, 2, 1), pad8)
    dest_c = jnp.pad(dest.reshape(n, TOP_K_INNER), ((0, 0), (0, LANES - TOP_K_INNER)), constant_values=-1)
    row = lambda i, j, *tables: (i, 0)
    tile3 = lambda i, j, *tables: (i, 0, 0)
    wmap = lambda i, j, *tables: (layer, j, 0, 0)
    n_rows = slots + MOE_CHUNK
    grid_spec = pltpu.PrefetchScalarGridSpec(
        num_scalar_prefetch=3,
        grid=(n_tiles, n_exp // epb),
        in_specs=[pl.BlockSpec((tm, d), row), pl.BlockSpec((tm, d), row),
                  pl.BlockSpec((None, 8, tm), tile3), pl.BlockSpec((None, 8, tm), tile3),
                  pl.BlockSpec((tm, LANES), row),
                  pl.BlockSpec((None, epb, d, dh), wmap), pl.BlockSpec((None, epb, d, dh), wmap),
                  pl.BlockSpec((None, epb, dh, d), wmap)],
        out_specs=pl.BlockSpec((tm, d), row),
        scratch_shapes=[pltpu.VMEM((n_rows, d), BF16), pltpu.VMEM((n_rows, d), BF16), pltpu.VMEM((n_rows, LANES), F32)],
    )
    return pl.pallas_call(
        functools.partial(_moe_routed_kernel, epb=epb, n_exp=n_exp, slots=slots),
        out_shape=jax.ShapeDtypeStruct((n, d), F32),
        grid_spec=grid_spec,
        compiler_params=_cparams("parallel", "arbitrary"),
        name="moe_routed",
    )(off.reshape(-1).astype(I32), (off + cnt_al).reshape(-1).astype(I32), n_chunks.reshape(-1).astype(I32),
      t_bf, x1, dest_t, w_t, dest_c, wg, wu, wd)


def _block_diag_rows(w):
    g, a, b = w.shape
    eye = jnp.eye(g, dtype=w.dtype)
    return jnp.einsum('gab,gh->gahb', w, eye).reshape(g * a, g * b)


def _seg_cumsum_matrix(rows, seg):
    i = np.arange(rows)[:, None]
    j = np.arange(rows)[None, :]
    l = (j <= i) & (i // seg == j // seg)
    return jnp.asarray(np.concatenate([l, l], axis=1), dtype=BF16)


def kernel(x_prompt, x_sample, cache_k, cache_v, state_ssm_re, state_ssm_im, state_pool, page_table, norm_mix_g, w_in, q_norm_g, k_norm_g, sb_bias, ssm_lambda_re, ssm_lambda_im, ssm_log_dt, ssm_b_re, ssm_b_im, ssm_c_re, ssm_c_im, ssm_d, ssm_glu_w, ssm_glu_b, pool_w, pool_scale, w_out, norm_ffn_g, router_group_w, router_group_b, router_expert_w, router_expert_b, expert_w_gate, expert_w_up, expert_w_down):
    depth = w_in.shape[0]
    bsz, seq, d = x_prompt.shape
    db, dseq, _ = x_sample.shape
    n_phys, page, nh, hd = cache_k.shape[1:]
    assert hd == HEAD_DIM
    aw = nh * hd
    n_grp_ssm, n_state, grp_ch = ssm_b_re.shape[1:]
    sw = n_grp_ssm * grp_ch
    ns = n_grp_ssm * n_state
    pool_buf = state_pool.shape[2]
    assert pool_buf < POOL_HIST and max(POOL_WINDOWS) - 1 <= pool_buf
    n_exp_grp = router_group_w.shape[2]
    n_exp = router_expert_w.shape[2]
    past_len = page_table.shape[1] * page
    np_tok = bsz * seq
    ns_tok = db * dseq
    t_rows = 128

    w_in_bf = w_in.astype(BF16)
    w_out_bf = w_out.astype(BF16)
    wg_bf = expert_w_gate.astype(BF16)
    wu_bf = expert_w_up.astype(BF16)
    wd_bf = expert_w_down.astype(BF16)
    glu_w_bf = ssm_glu_w.astype(BF16)
    ones_bd = jnp.asarray(np.kron(np.eye(nh), np.ones((hd, hd))), dtype=BF16)
    cache_kt = cache_k.transpose(0, 1, 3, 4, 2).reshape(depth, n_phys, aw, page)
    cache_vt = cache_v.transpose(0, 1, 3, 4, 2).reshape(depth, n_phys, aw, page)
    head_eye = jnp.eye(nh, dtype=BF16)
    l2_prompt = _seg_cumsum_matrix(t_rows, t_rows)
    l2_sample = _seg_cumsum_matrix(ns_tok, dseq)
    seg_rows = POOL_HIST + dseq

    xp = x_prompt.reshape(np_tok, d)
    xs = x_sample.reshape(ns_tok, d)
    outs = {k: [] for k in ("kp", "vp", "hrp", "hip", "pbp", "ks", "vs", "hrs", "his", "pbs")}

    for l in range(depth):
        g_mix = norm_mix_g[l][None, :]
        qg_t = jnp.tile(q_norm_g[l], nh)[None, :]
        kg_t = jnp.tile(k_norm_g[l], nh)[None, :]
        bias_rows = jnp.broadcast_to(sb_bias[l][:, None], (nh, LANES))
        bias_rows_s = jnp.broadcast_to(jnp.repeat(sb_bias[l], dseq)[:, None], (nh * dseq, LANES))
        lr = ssm_lambda_re[l].reshape(1, ns)
        li = ssm_lambda_im[l].reshape(1, ns)
        ldt = jnp.broadcast_to(ssm_log_dt[l][:, None], (n_grp_ssm, n_state)).reshape(1, ns)
        bre_bd = _block_diag_rows(jnp.swapaxes(ssm_b_re[l], 1, 2))
        bim_bd = _block_diag_rows(jnp.swapaxes(ssm_b_im[l], 1, 2))
        cre_bd = _block_diag_rows(jnp.swapaxes(ssm_c_re[l], 1, 2))
        cim_bd = _block_diag_rows(jnp.swapaxes(ssm_c_im[l], 1, 2))
        prep = _ssm_prep(lr, li, ldt, bre_bd, bim_bd, cre_bd, cim_bd, t_rows)
        prep_s = tuple(prep[:3]) + tuple(jnp.tile(tb[:dseq], (db, 1)) for tb in prep[3:])
        dskip = ssm_d[l][None, :]
        glub = ssm_glu_b[l][None, :]
        pw_bd = _block_diag_rows(pool_w[l]).astype(BF16)
        pscale = pool_scale[l][None, :]
        g_ffn = norm_ffn_g[l][None, :]
        rw = jnp.zeros((d, LANES), F32).at[:, :n_exp].set(router_expert_w[l])
        rw = rw.at[:, n_exp:n_exp + n_exp_grp].set(router_group_w[l])
        rw_hi = rw.astype(BF16)
        rw_lo = (rw - rw_hi.astype(F32)).astype(BF16)
        rb = jnp.zeros((1, LANES), F32).at[0, :n_exp].set(router_expert_b[l])
        rb = rb.at[0, n_exp:n_exp + n_exp_grp].set(router_group_b[l])

        q_bf, k32, v32, k_bf, v_bf, us, up = _inproj(xp, g_mix, w_in_bf[l], qg_t, kg_t, ones_bd, tm=512)
        attn = _attn_prompt(q_bf, k_bf, v_bf, bias_rows, bsz, seq, tq=1024)
        sso, poo, hlast = _mixers_prompt(us, up, prep, l2_prompt, dskip, glu_w_bf[l], glub, pw_bd, pscale,
                                         bsz, seq, t_rows)
        x1, t_bf, gates = _outproj(xp, attn, sso, poo, w_out_bf[l], g_ffn, rw_hi, rw_lo, rb, n_exp, n_exp_grp, tm=512)
        xp = _moe_routed(t_bf, gates, x1, wg_bf, wu_bf, wd_bf, l, tm=1024, epb=4)
        outs["kp"].append(k32.reshape(np_tok // page, page, nh, hd))
        outs["vp"].append(v32.reshape(np_tok // page, page, nh, hd))
        outs["hrp"].append(hlast[:, 0, :ns].reshape(bsz, n_grp_ssm, n_state))
        outs["hip"].append(hlast[:, 0, ns:].reshape(bsz, n_grp_ssm, n_state))
        outs["pbp"].append(up.reshape(bsz, seq, sw)[:, seq - pool_buf:, :])

        q_bf, k32, v32, k_bf, v_bf, us, up = _inproj(xs, g_mix, w_in_bf[l], qg_t, kg_t, ones_bd, tm=ns_tok)
        qbd = jnp.einsum('bthd,hg->bhtgd', q_bf.reshape(db, dseq, nh, hd), head_eye).reshape(db, nh * dseq, aw)
        pad_rows = ((0, 0), (0, LANES - dseq), (0, 0))
        attn = _attn_sample(qbd, jnp.pad(k32.reshape(db, dseq, aw), pad_rows), jnp.pad(v32.reshape(db, dseq, aw), pad_rows),
                            cache_kt, cache_vt, page_table, bias_rows_s, l, n_pages_step=16, nq=dseq)
        ext = jnp.concatenate([jnp.zeros((db, POOL_HIST - pool_buf, sw), F32), state_pool[l],
                               up.reshape(db, dseq, sw)], axis=1)
        h0 = jnp.concatenate([state_ssm_re[l].reshape(db, ns), state_ssm_im[l].reshape(db, ns)], axis=1)
        h0rep = jnp.repeat(h0, dseq, axis=0)
        sso, poo_ext, hall = _mixers_sample(us, ext.reshape(db * seg_rows, sw), h0rep, prep_s, l2_sample, dskip,
                                            glu_w_bf[l], glub, pw_bd, pscale, seg_rows, past_len)
        poo = poo_ext.reshape(db, seg_rows, sw)[:, POOL_HIST:, :].reshape(ns_tok, sw)
        x1, t_bf, gates = _outproj(xs, attn.reshape(ns_tok, aw), sso, poo, w_out_bf[l], g_ffn, rw_hi, rw_lo, rb,
                                   n_exp, n_exp_grp, tm=ns_tok)
        xs = _moe(t_bf, gates, x1, wg_bf, wu_bf, wd_bf, l, tm=ns_tok, epb=2)
        hl = hall.reshape(db, dseq, 2 * ns)[:, dseq - 1, :]
        outs["ks"].append(k32.reshape(db, dseq, nh, hd))
        outs["vs"].append(v32.reshape(db, dseq, nh, hd))
        outs["hrs"].append(hl[:, :ns].reshape(db, n_grp_ssm, n_state))
        outs["his"].append(hl[:, ns:].reshape(db, n_grp_ssm, n_state))
        outs["pbs"].append(ext[:, seg_rows - pool_buf:, :])

    st = {k: jnp.stack(v) for k, v in outs.items()}
    return (xp.reshape(bsz, seq, d), xs.reshape(db, dseq, d), st["kp"], st["vp"], st["hrp"], st["hip"], st["pbp"],
            st["ks"], st["vs"], st["hrs"], st["his"], st["pbs"])
```

```python
import functools
import math

import numpy as np
import jax
import jax.numpy as jnp
from jax import lax
from jax.experimental import pallas as pl
from jax.experimental.pallas import tpu as pltpu

F32 = jnp.float32
BF16 = jnp.bfloat16
I32 = jnp.int32

EPS = 1e-6
HEAD_DIM = 64
POOL_WINDOWS = (2, 4, 8, 16)
POOL_HIST = 16
TOP_K_INNER = 2
LANES = 128
VMEM_LIMIT = 56 * 1024 * 1024

NEG_INF = float("-inf")
LOG2E = math.log2(math.e)
ROUTE_LANE = 64
MOE_ROW_ALIGN = 16
MOE_CHUNK = 128


def _cparams(*sem):
    return pltpu.CompilerParams(dimension_semantics=sem, vmem_limit_bytes=VMEM_LIMIT)


def _split_bf16(x):
    hi = x.astype(BF16)
    lo = (x - hi.astype(F32)).astype(BF16)
    return hi, lo


def _dot(a, b):
    return jnp.dot(a, b, preferred_element_type=F32)


def _dot_nt(a, b):
    return lax.dot_general(a, b, (((1,), (1,)), ((), ())), preferred_element_type=F32)


def _dot_tn(a, b):
    return lax.dot_general(a, b, (((0,), (0,)), ((), ())), preferred_element_type=F32)


def _neg_abs(z):
    bits = lax.bitcast_convert_type(z, jnp.uint32) | jnp.uint32(0x80000000)
    return lax.bitcast_convert_type(bits, F32)


def _neg_softplus2(z):
    return -(jnp.maximum(z, 0.0) + jnp.log2(1.0 + jnp.exp2(_neg_abs(z))))


def _inproj_kernel(x_ref, g_ref, w_ref, qg_ref, kg_ref, ones_ref,
                   q_ref, k_ref, v_ref, kb_ref, vb_ref, us_ref, up_ref, *, aw, sw):
    x = x_ref[...]
    ms = jnp.mean(x * x, axis=-1, keepdims=True)
    h = (x * lax.rsqrt(ms + EPS) * g_ref[...]).astype(BF16)
    proj = _dot(h, w_ref[...])
    ones_bd = ones_ref[...]

    def headnorm(t, gain):
        hi, lo = _split_bf16(t * t)
        ss = _dot(hi, ones_bd) + _dot(lo, ones_bd)
        return t * lax.rsqrt(ss * (1.0 / HEAD_DIM) + EPS) * gain

    q = headnorm(proj[:, :aw], qg_ref[...])
    k = headnorm(proj[:, aw:2 * aw], kg_ref[...])
    v = proj[:, 2 * aw:3 * aw]
    q_ref[...] = (q * (LOG2E * HEAD_DIM ** -0.5)).astype(BF16)
    k_ref[...] = k
    v_ref[...] = v
    kb_ref[...] = k.astype(BF16)
    vb_ref[...] = v.astype(BF16)
    us_ref[...] = proj[:, 3 * aw:3 * aw + sw]
    up_ref[...] = proj[:, 3 * aw + sw:]


def _inproj(x2, g_mix, w_in_bf, qg_t, kg_t, ones_bd, tm):
    n, d = x2.shape
    aw = ones_bd.shape[0]
    sw = (w_in_bf.shape[1] - 3 * aw) // 2
    row = lambda i: (i, 0)
    const = lambda i: (0, 0)
    sds = jax.ShapeDtypeStruct
    return pl.pallas_call(
        functools.partial(_inproj_kernel, aw=aw, sw=sw),
        out_shape=[sds((n, aw), BF16), sds((n, aw), F32), sds((n, aw), F32),
                   sds((n, aw), BF16), sds((n, aw), BF16), sds((n, sw), F32), sds((n, sw), F32)],
        grid=(n // tm,),
        in_specs=[pl.BlockSpec((tm, d), row), pl.BlockSpec((1, d), const),
                  pl.BlockSpec(w_in_bf.shape, const), pl.BlockSpec((1, aw), const),
                  pl.BlockSpec((1, aw), const), pl.BlockSpec((aw, aw), const)],
        out_specs=[pl.BlockSpec((tm, aw), row)] * 5 + [pl.BlockSpec((tm, sw), row)] * 2,
        compiler_params=_cparams("parallel"),
        name="inproj",
    )(x2, g_mix, w_in_bf, qg_t, kg_t, ones_bd)


def _attn_prompt_kernel(qi_ref, kj_ref, q_ref, k_ref, v_ref, uu_ref, bias_ref, o_ref,
                        r_ref, acc_ref, *, tq, tkb, sub, rc):
    s = pl.program_id(2)
    hp = pl.program_id(1)
    qi = qi_ref[s]
    kj = kj_ref[s]
    is_diag = kj == qi

    @pl.when(is_diag)
    def _():
        r_ref[...] = jnp.zeros_like(r_ref)
        acc_ref[...] = jnp.zeros_like(acc_ref)

    n_sub = tkb // sub
    n_rc = tq // rc
    lane_k = lax.broadcasted_iota(I32, (sub, LANES), 1)
    d_idx = lax.broadcasted_iota(I32, (rc, sub), 1) - lax.broadcasted_iota(I32, (rc, sub), 0)
    biases = (bias_ref[pl.ds(2 * hp, 1), :] * LOG2E, bias_ref[pl.ds(2 * hp + 1, 1), :] * LOG2E)
    uu = uu_ref[...]
    zero_bf = jnp.zeros((), BF16)
    head_lanes = (lane_k < HEAD_DIM, lane_k >= HEAD_DIM)

    def run(tiles, masked):
        kv = {}

        def kv_cat(c):
            if c not in kv:
                kc = k_ref[pl.ds(c * sub, sub), :]
                vc = v_ref[pl.ds(c * sub, sub), :]
                kv[c] = (jnp.concatenate([jnp.where(m, kc, zero_bf) for m in head_lanes], axis=0),
                         jnp.concatenate([jnp.where(m, vc, zero_bf) for m in head_lanes], axis=0))
            return kv[c]

        def scores(c, r):
            return _dot_nt(q_ref[pl.ds(r * rc, rc), :], kv_cat(c)[0])

        def suffix_sums(c, r, z2):
            valid = d_idx < (r * rc - c * sub) if masked else None
            zs, cums = [], []
            for hd in range(2):
                z = z2[:, hd * sub:(hd + 1) * sub] + biases[hd]
                sp = jnp.maximum(z, 0.0) + jnp.log2(1.0 + jnp.exp2(_neg_abs(z)))
                if masked:
                    sp = jnp.where(valid, sp, 0.0)
                zs.append(z)
                cums.append(_dot(sp.astype(BF16), uu[:sub]))
            return zs, cums, valid

        def weights_and_values(c, r, zs, cums, valid):
            rows = pl.ds(r * rc, rc)
            ps = []
            for hd in range(2):
                rr = r_ref[hd, rows, :]
                p = jnp.exp2(zs[hd] - (cums[hd] + rr))
                if masked:
                    p = jnp.where(valid, p, 0.0)
                r_ref[hd, rows, :] = rr + jnp.broadcast_to(cums[hd][:, 0:1], rr.shape)
                ps.append(p.astype(BF16))
            acc_ref[rows, :] += _dot(jnp.concatenate(ps, axis=1), kv_cat(c)[1])

        z2 = scores(*tiles[0])
        for n, (c, r) in enumerate(tiles):
            mid = suffix_sums(c, r, z2)
            if n + 1 < len(tiles):
                z2 = scores(*tiles[n + 1])
            weights_and_values(c, r, *mid)

    @pl.when(is_diag)
    def _():
        run([(c, r) for c in range(n_sub - 1, -1, -1) for r in range((c * sub) // rc, n_rc)], True)

    @pl.when(jnp.logical_not(is_diag))
    def _():
        run([(c, r) for c in range(n_sub - 1, -1, -1) for r in range(n_rc)], False)

    @pl.when(kj == 0)
    def _():
        o_ref[...] = acc_ref[...].astype(o_ref.dtype)


def _suffix_matrix(sub):
    j = np.arange(2 * sub)[:, None] % sub
    s = np.arange(sub)[None, :]
    return jnp.asarray(j >= s, dtype=BF16)


def _attn_prompt(q_bf, k_bf, v_bf, bias_rows, bsz, seq, tq):
    n, aw = q_bf.shape
    tkb = tq
    sub = LANES
    rc = 512
    nq = seq // tq
    qi_tbl = np.concatenate([np.full(i + 1, i) for i in range(nq)]).astype(np.int32)
    kj_tbl = np.concatenate([np.arange(i, -1, -1) for i in range(nq)]).astype(np.int32)
    n_steps = qi_tbl.shape[0]
    n_hp = aw // LANES
    uu = _suffix_matrix(sub)
    grid_spec = pltpu.PrefetchScalarGridSpec(
        num_scalar_prefetch=2,
        grid=(bsz, n_hp, n_steps),
        in_specs=[
            pl.BlockSpec((tq, LANES), lambda b, h, s, qi, kj: (b * nq + qi[s], h)),
            pl.BlockSpec((tkb, LANES), lambda b, h, s, qi, kj: (b * nq + kj[s], h)),
            pl.BlockSpec((tkb, LANES), lambda b, h, s, qi, kj: (b * nq + kj[s], h)),
            pl.BlockSpec(uu.shape, lambda b, h, s, qi, kj: (0, 0)),
            pl.BlockSpec(bias_rows.shape, lambda b, h, s, qi, kj: (0, 0)),
        ],
        out_specs=pl.BlockSpec((tq, LANES), lambda b, h, s, qi, kj: (b * nq + qi[s], h)),
        scratch_shapes=[pltpu.VMEM((2, tq, LANES), F32), pltpu.VMEM((tq, LANES), F32)],
    )
    return pl.pallas_call(
        functools.partial(_attn_prompt_kernel, tq=tq, tkb=tkb, sub=sub, rc=rc),
        out_shape=jax.ShapeDtypeStruct((n, aw), BF16),
        grid_spec=grid_spec,
        compiler_params=_cparams("parallel", "parallel", "arbitrary"),
        name="attn_prompt",
    )(jnp.asarray(qi_tbl), jnp.asarray(kj_tbl), q_bf, k_bf, v_bf, uu, bias_rows)


def _attn_sample_kernel(pt_ref, q_ref, kn_ref, vn_ref, bias_ref, uu_ref, *rest, n_pages_step, nq, nh):
    k_refs = rest[:n_pages_step]
    v_refs = rest[n_pages_step:2 * n_pages_step]
    o_ref = rest[2 * n_pages_step]
    r_ref, acc_ref = rest[2 * n_pages_step + 1:]
    del pt_ref
    j = pl.program_id(1)
    nrow = nh * nq
    bias = bias_ref[...] * LOG2E
    qbd = q_ref[...]
    uu = uu_ref[...]

    def softplus2(z):
        return jnp.maximum(z, 0.0) + jnp.log2(1.0 + jnp.exp2(_neg_abs(z)))

    @pl.when(j == 0)
    def _():
        z = _dot_nt(qbd, kn_ref[...].astype(BF16)) + bias
        lane_tok = lax.broadcasted_iota(I32, (nrow, LANES), 1)
        row_q = lax.broadcasted_iota(I32, (nrow, LANES), 0) % nq
        valid = lane_tok < row_q
        hi, lo = _split_bf16(jnp.where(valid, softplus2(z), 0.0))
        ct = _dot(jnp.concatenate([hi, lo], axis=1), uu)
        p = jnp.where(valid, jnp.exp2(z - ct[:, :LANES]), 0.0)
        r_ref[...] = ct[:, LANES:]
        acc_ref[...] = _dot(p.astype(BF16), vn_ref[...].astype(BF16))

    pages = range(n_pages_step - 1, -1, -1)
    zz = {i: _dot(qbd, k_refs[i][...].astype(BF16)) + bias for i in pages}
    hi, lo = _split_bf16(jnp.concatenate([softplus2(zz[i]) for i in pages], axis=0))
    ct = _dot(jnp.concatenate([hi, lo], axis=1), uu)
    rr = r_ref[...]
    ps = {}
    for n, i in enumerate(pages):
        ps[i] = jnp.exp2(zz[i] - (ct[n * nrow:(n + 1) * nrow, :LANES] + rr)).astype(BF16)
        rr = rr + ct[n * nrow:(n + 1) * nrow, LANES:]
    acc = acc_ref[...]
    for i in pages:
        acc = acc + _dot_nt(ps[i], v_refs[i][...].astype(BF16))
    r_ref[...] = rr
    acc_ref[...] = acc

    @pl.when(j == pl.num_programs(1) - 1)
    def _():
        row_head = lax.broadcasted_iota(I32, acc.shape, 0) // nq
        col_head = lax.broadcasted_iota(I32, acc.shape, 1) // HEAD_DIM
        own = jnp.where(row_head == col_head, acc, 0.0)
        out = own[0:nq, :]
        for h in range(1, nh):
            out = out + own[h * nq:(h + 1) * nq, :]
        o_ref[...] = out.astype(o_ref.dtype)


def _attn_sample(qbd, kn_pad, vn_pad, cache_kt, cache_vt, page_table, bias_rows, layer, n_pages_step, nq):
    db, nrow, aw = qbd.shape
    nh = nrow // nq
    page = cache_kt.shape[3]
    assert page == LANES
    n_pages = page_table.shape[1]
    n_steps = n_pages // n_pages_step
    jj = np.arange(2 * LANES)[:, None] % LANES
    ss = np.arange(2 * LANES)[None, :]
    uu = jnp.asarray((ss >= LANES) | (jj >= ss), dtype=BF16)

    def page_map(i):
        return lambda b, j, pt: (layer, pt[b, (n_steps - 1 - j) * n_pages_step + i], 0, 0)

    seq_map = lambda b, j, pt: (b, 0, 0)
    const = lambda b, j, pt: (0, 0)
    page_spec = [pl.BlockSpec((None, None, aw, page), page_map(i)) for i in range(n_pages_step)]
    grid_spec = pltpu.PrefetchScalarGridSpec(
        num_scalar_prefetch=1,
        grid=(db, n_steps),
        in_specs=[pl.BlockSpec((None, nrow, aw), seq_map), pl.BlockSpec((None, LANES, aw), seq_map),
                  pl.BlockSpec((None, LANES, aw), seq_map), pl.BlockSpec(bias_rows.shape, const),
                  pl.BlockSpec(uu.shape, const)] + page_spec + page_spec,
        out_specs=pl.BlockSpec((None, nq, aw), seq_map),
        scratch_shapes=[pltpu.VMEM((nrow, LANES), F32), pltpu.VMEM((nrow, aw), F32)],
    )
    return pl.pallas_call(
        functools.partial(_attn_sample_kernel, n_pages_step=n_pages_step, nq=nq, nh=nh),
        out_shape=jax.ShapeDtypeStruct((db, nq, aw), BF16),
        grid_spec=grid_spec,
        compiler_params=_cparams("parallel", "arbitrary"),
        name="attn_sample",
    )(page_table, qbd, kn_pad, vn_pad, bias_rows, uu, *([cache_kt] * n_pages_step), *([cache_vt] * n_pages_step))


def _ssm_prep_kernel(lr_ref, li_ref, ldt_ref, bre_ref, bim_ref, cre_ref, cim_ref,
                     a_ref, bbar_ref, cbd_ref, pre_ref, pim_ref, qre_ref, qim_ref, *, t_rows):
    lr = lr_ref[...]
    li = li_ref[...]
    dt = jnp.exp(ldt_ref[...])
    mag = jnp.exp(lr * dt)
    ar = mag * jnp.cos(li * dt)
    ai = mag * jnp.sin(li * dt)
    er = ar - 1.0
    den = lr * lr + li * li
    fr = (er * lr + ai * li) / den
    fi = (ai * lr - er * li) / den
    a_ref[0:1, :] = ar
    a_ref[1:2, :] = ai
    bre = bre_ref[...]
    bim = bim_ref[...]
    ns = lr.shape[1]
    bbar_ref[:, :ns] = (fr * bre - fi * bim).astype(BF16)
    bbar_ref[:, ns:] = (fr * bim + fi * bre).astype(BF16)
    cbd_ref[:ns, :] = cre_ref[...].astype(BF16)
    cbd_ref[ns:, :] = (-cim_ref[...]).astype(BF16)
    t = lax.broadcasted_iota(I32, (t_rows, ns), 0).astype(F32)
    grow = jnp.exp(t * (lr * dt))
    shrink = jnp.exp(-t * (lr * dt))
    ang = t * (li * dt)
    c = jnp.cos(ang)
    s = jnp.sin(ang)
    pre_ref[...] = grow * c
    pim_ref[...] = grow * s
    qre_ref[...] = shrink * c
    qim_ref[...] = -(shrink * s)


def _ssm_prep(lr, li, ldt, bre_bd, bim_bd, cre_bd, cim_bd, t_rows):
    ns = lr.shape[1]
    sw = bre_bd.shape[0]
    sds = jax.ShapeDtypeStruct
    return pl.pallas_call(
        functools.partial(_ssm_prep_kernel, t_rows=t_rows),
        out_shape=[sds((2, ns), F32), sds((sw, 2 * ns), BF16), sds((2 * ns, sw), BF16)]
        + [sds((t_rows, ns), F32)] * 4,
        compiler_params=pltpu.CompilerParams(vmem_limit_bytes=VMEM_LIMIT),
        name="ssm_prep",
    )(lr, li, ldt, bre_bd, bim_bd, cre_bd, cim_bd)


def _ssm_compute(u32, bbar, cbd, a, pre, pim, qre, qim, l2, hp_re, hp_im, dskip, gluw, glub):
    ns = pre.shape[1]
    bu = _dot(u32.astype(BF16), bbar)
    bur, bui = bu[:, :ns], bu[:, ns:]
    xr = qre * bur - qim * bui
    xi = qre * bui + qim * bur
    x_hi, x_lo = _split_bf16(jnp.concatenate([xr, xi], axis=1))
    s = _dot(l2, jnp.concatenate([x_hi, x_lo], axis=0))
    ar, ai = a[0:1, :], a[1:2, :]
    sr = s[:, :ns] + (ar * hp_re - ai * hp_im)
    si = s[:, ns:] + (ar * hp_im + ai * hp_re)
    hr = pre * sr - pim * si
    hi_ = pre * si + pim * sr
    hcat = jnp.concatenate([hr, hi_], axis=1)
    y = _dot(hcat.astype(BF16), cbd) + dskip * u32
    g = 0.5 * y * (1.0 + jnp.tanh(math.sqrt(2.0 / math.pi) * (y + 0.044715 * (y * y * y))))
    gate = 1.0 / (1.0 + jnp.exp(-(_dot(g.astype(BF16), gluw) + glub)))
    return g * gate, hcat


def _pool_compute(ext, pos, pw_bd, pscale):
    s2 = ext + pltpu.roll(ext, 1, 0)
    s4 = s2 + pltpu.roll(s2, 2, 0)
    s8 = s4 + pltpu.roll(s4, 4, 0)
    s16 = s8 + pltpu.roll(s8, 8, 0)
    grp = lax.broadcasted_iota(I32, ext.shape, 1) // (ext.shape[1] // len(POOL_WINDOWS))
    win = jnp.where(grp == 0, s2, jnp.where(grp == 1, s4, jnp.where(grp == 2, s8, s16)))
    w = jnp.where(grp == 0, POOL_WINDOWS[0],
                  jnp.where(grp == 1, POOL_WINDOWS[1], jnp.where(grp == 2, POOL_WINDOWS[2], POOL_WINDOWS[3])))
    cnt = jnp.maximum(jnp.minimum(pos + 1, w), 1).astype(F32)
    d = win / cnt - ext
    return _dot(d.astype(BF16), pw_bd) * pscale


def _mixers_prompt_kernel(us_ref, up_ref, bbar_ref, cbd_ref, a_ref, pre_ref, pim_ref, qre_ref, qim_ref, l2_ref,
                          dskip_ref, gluw_ref, glub_ref, pw_ref, ps_ref,
                          so_ref, po_ref, hl_ref, carry_ref, ext_ref, *, t_rows):
    c = pl.program_id(1)
    ns = pre_ref.shape[1]

    @pl.when(c == 0)
    def _():
        carry_ref[...] = jnp.zeros_like(carry_ref)
        ext_ref[0:POOL_HIST, :] = jnp.zeros((POOL_HIST, ext_ref.shape[1]), F32)

    carry = carry_ref[...]
    out, hcat = _ssm_compute(us_ref[...], bbar_ref[...], cbd_ref[...], a_ref[...], pre_ref[...], pim_ref[...],
                             qre_ref[...], qim_ref[...], l2_ref[...], carry[:, :ns], carry[:, ns:],
                             dskip_ref[...], gluw_ref[...], glub_ref[...])
    so_ref[...] = out.astype(so_ref.dtype)
    last = hcat[t_rows - 1:t_rows, :]
    carry_ref[...] = last
    hl_ref[...] = last

    ext_ref[POOL_HIST:, :] = up_ref[...]
    ext = ext_ref[...]
    pos = c * t_rows - POOL_HIST + lax.broadcasted_iota(I32, (t_rows + POOL_HIST, 1), 0)
    pooled = _pool_compute(ext, pos, pw_ref[...], ps_ref[...])
    po_ref[...] = pooled[POOL_HIST:, :].astype(po_ref.dtype)
    ext_ref[0:POOL_HIST, :] = ext[t_rows:, :]


def _mixers_prompt(us, up, prep, l2, dskip, gluw, glub, pw_bd, pscale, bsz, seq, t_rows):
    a, bbar, cbd, pre, pim, qre, qim = prep
    n, sw = us.shape
    ns = pre.shape[1]
    nc = seq // t_rows
    row = lambda b, c: (b * nc + c, 0)
    const = lambda b, c: (0, 0)
    full = lambda arr: pl.BlockSpec(arr.shape, const)
    sds = jax.ShapeDtypeStruct
    return pl.pallas_call(
        functools.partial(_mixers_prompt_kernel, t_rows=t_rows),
        out_shape=[sds((n, sw), BF16), sds((n, sw), BF16), sds((bsz, 1, 2 * ns), F32)],
        grid=(bsz, nc),
        in_specs=[pl.BlockSpec((t_rows, sw), row), pl.BlockSpec((t_rows, sw), row),
                  full(bbar), full(cbd), full(a), full(pre), full(pim), full(qre), full(qim), full(l2),
                  full(dskip), full(gluw), full(glub), full(pw_bd), full(pscale)],
        out_specs=[pl.BlockSpec((t_rows, sw), row), pl.BlockSpec((t_rows, sw), row),
                   pl.BlockSpec((None, 1, 2 * ns), lambda b, c: (b, 0, 0))],
        scratch_shapes=[pltpu.VMEM((1, 2 * ns), F32), pltpu.VMEM((t_rows + POOL_HIST, sw), F32)],
        compiler_params=_cparams("parallel", "arbitrary"),
        name="mixers_prompt",
    )(us, up, bbar, cbd, a, pre, pim, qre, qim, l2, dskip, gluw, glub, pw_bd, pscale)


def _mixers_sample_kernel(us_ref, ext_ref, h0_ref, bbar_ref, cbd_ref, a_ref, pre_ref, pim_ref, qre_ref, qim_ref,
                          l2_ref, dskip_ref, gluw_ref, glub_ref, pw_ref, ps_ref,
                          so_ref, po_ref, h_ref, *, seg_rows, start_pos):
    ns = pre_ref.shape[1]
    h0 = h0_ref[...]
    out, hcat = _ssm_compute(us_ref[...], bbar_ref[...], cbd_ref[...], a_ref[...], pre_ref[...], pim_ref[...],
                             qre_ref[...], qim_ref[...], l2_ref[...], h0[:, :ns], h0[:, ns:],
                             dskip_ref[...], gluw_ref[...], glub_ref[...])
    so_ref[...] = out.astype(so_ref.dtype)
    h_ref[...] = hcat
    ext = ext_ref[...]
    pos = start_pos - POOL_HIST + lax.broadcasted_iota(I32, (ext.shape[0], 1), 0) % seg_rows
    po_ref[...] = _pool_compute(ext, pos, pw_ref[...], ps_ref[...]).astype(po_ref.dtype)


def _mixers_sample(us, ext, h0rep, prep_s, l2, dskip, gluw, glub, pw_bd, pscale, seg_rows, start_pos):
    a, bbar, cbd, pre, pim, qre, qim = prep_s
    n, sw = us.shape
    ns = pre.shape[1]
    sds = jax.ShapeDtypeStruct
    return pl.pallas_call(
        functools.partial(_mixers_sample_kernel, seg_rows=seg_rows, start_pos=start_pos),
        out_shape=[sds((n, sw), BF16), sds(ext.shape, BF16), sds((n, 2 * ns), F32)],
        compiler_params=pltpu.CompilerParams(vmem_limit_bytes=VMEM_LIMIT),
        name="mixers_sample",
    )(us, ext, h0rep, bbar, cbd, a, pre, pim, qre, qim, l2, dskip, gluw, glub, pw_bd, pscale)


def _outproj_kernel(x_ref, at_ref, so_ref, po_ref, wo_ref, g_ref, rwh_ref, rwl_ref, rb_ref,
                    x1_ref, t_ref, gates_ref, *, n_exp, n_grp):
    mix = jnp.concatenate([at_ref[...], so_ref[...], po_ref[...]], axis=1)
    x1 = x_ref[...] + _dot(mix, wo_ref[...])
    x1_ref[...] = x1
    ms = jnp.mean(x1 * x1, axis=-1, keepdims=True)
    t = x1 * lax.rsqrt(ms + EPS) * g_ref[...]
    t_ref[...] = t.astype(BF16)
    t_hi, t_lo = _split_bf16(t)
    rwh = rwh_ref[...]
    logits = _dot(t_hi, rwh) + _dot(t_hi, rwl_ref[...]) + _dot(t_lo, rwh) + rb_ref[...]
    lane = lax.broadcasted_iota(I32, logits.shape, 1)
    big = jnp.int32(2 ** 30)
    epg = n_exp // n_grp
    is_g = (lane >= n_exp) & (lane < n_exp + n_grp)
    gl = jnp.where(is_g, logits, NEG_INF)
    gmax = jnp.max(gl, axis=1, keepdims=True)
    gidx = jnp.min(jnp.where(gl == gmax, lane - n_exp, big), axis=1, keepdims=True)
    g_w = 1.0 / jnp.sum(jnp.where(is_g, jnp.exp(gl - gmax), 0.0), axis=1, keepdims=True)
    in_grp = (lane < n_exp) & (lane // epg == gidx)
    el = jnp.where(in_grp, logits, NEG_INF)
    m1 = jnp.max(el, axis=1, keepdims=True)
    i1 = jnp.min(jnp.where(el == m1, lane, big), axis=1, keepdims=True)
    el2 = jnp.where(lane == i1, NEG_INF, el)
    m2 = jnp.max(el2, axis=1, keepdims=True)
    i2 = jnp.min(jnp.where(el2 == m2, lane, big), axis=1, keepdims=True)
    e21 = jnp.exp(m2 - m1)
    w1 = 1.0 / (1.0 + e21)
    w2 = e21 * w1
    gates = jnp.where(lane == i1, w1 * g_w, jnp.where(lane == i2, w2 * g_w, 0.0))
    route = jnp.where(lane == ROUTE_LANE, i1.astype(F32),
                      jnp.where(lane == ROUTE_LANE + 1, i2.astype(F32),
                                jnp.where(lane == ROUTE_LANE + 2, w1 * g_w,
                                          jnp.where(lane == ROUTE_LANE + 3, w2 * g_w, gates))))
    gates_ref[...] = route


def _outproj(x2, attn, sso, poo, wo_bf, g_ffn, rw_hi, rw_lo, rb, n_exp, n_grp, tm):
    n, d = x2.shape
    aw, sw = attn.shape[1], sso.shape[1]
    row = lambda i: (i, 0)
    const = lambda i: (0, 0)
    sds = jax.ShapeDtypeStruct
    return pl.pallas_call(
        functools.partial(_outproj_kernel, n_exp=n_exp, n_grp=n_grp),
        out_shape=[sds((n, d), F32), sds((n, d), BF16), sds((n, LANES), F32)],
        grid=(n // tm,),
        in_specs=[pl.BlockSpec((tm, d), row), pl.BlockSpec((tm, aw), row), pl.BlockSpec((tm, sw), row),
                  pl.BlockSpec((tm, sw), row), pl.BlockSpec(wo_bf.shape, const), pl.BlockSpec((1, d), const),
                  pl.BlockSpec(rw_hi.shape, const), pl.BlockSpec(rw_lo.shape, const), pl.BlockSpec((1, LANES), const)],
        out_specs=[pl.BlockSpec((tm, d), row), pl.BlockSpec((tm, d), row), pl.BlockSpec((tm, LANES), row)],
        compiler_params=_cparams("parallel"),
        name="outproj",
    )(x2, attn, sso, poo, wo_bf, g_ffn, rw_hi, rw_lo, rb)


def _moe_kernel(t_ref, gates_ref, x1_ref, wg_ref, wu_ref, wd_ref, o_ref, *, epb):
    j = pl.program_id(1)

    @pl.when(j == 0)
    def _():
        o_ref[...] = x1_ref[...]

    t = t_ref[...]
    g_hi, g_lo = _split_bf16(gates_ref[...])
    g2 = jnp.concatenate([g_hi, g_lo], axis=1)
    dh = wg_ref.shape[2]
    sel_row = lax.broadcasted_iota(I32, (2 * LANES, dh), 0) % LANES
    acc = jnp.zeros(o_ref.shape, F32)
    for i in range(epb):
        e = j * epb + i
        gb = _dot(g2, (sel_row == e).astype(BF16))
        a = _dot(t, wg_ref[i])
        b = _dot(t, wu_ref[i])
        hh = (a / (1.0 + jnp.exp(-a))) * b * gb
        acc = acc + _dot(hh.astype(BF16), wd_ref[i])
    o_ref[...] += acc


def _moe(t_bf, gates, x1, wg, wu, wd, layer, tm, epb):
    n, d = x1.shape
    n_exp, _, dh = wg.shape[1:]
    row = lambda i, j: (i, 0)
    return pl.pallas_call(
        functools.partial(_moe_kernel, epb=epb),
        out_shape=jax.ShapeDtypeStruct((n, d), F32),
        grid=(n // tm, n_exp // epb),
        in_specs=[pl.BlockSpec((tm, d), row), pl.BlockSpec((tm, LANES), row), pl.BlockSpec((tm, d), row),
                  pl.BlockSpec((None, epb, d, dh), lambda i, j: (layer, j, 0, 0)),
                  pl.BlockSpec((None, epb, d, dh), lambda i, j: (layer, j, 0, 0)),
                  pl.BlockSpec((None, epb, dh, d), lambda i, j: (layer, j, 0, 0))],
        out_specs=pl.BlockSpec((tm, d), row),
        compiler_params=_cparams("parallel", "arbitrary"),
        name="moe",
    )(t_bf, gates, x1, wg, wu, wd)


def _moe_routed_kernel(off_ref, end_ref, nch_ref, t_ref, x1_ref, route_ref, offv_ref, tri_ref,
                       wg_ref, wu_ref, wd_ref, o_ref, xs_ref, ys_ref, gs_ref, dc_ref, *, epb, n_exp, slots):
    i = pl.program_id(0)
    j = pl.program_id(1)
    tm = t_ref.shape[0]

    @pl.when(j == 0)
    def _():
        t = t_ref[...]
        route = route_ref[...]
        lane = lax.broadcasted_iota(I32, (tm, LANES), 1)
        e1 = route[:, ROUTE_LANE:ROUTE_LANE + 1].astype(I32)
        e2 = route[:, ROUTE_LANE + 1:ROUTE_LANE + 2].astype(I32)
        pick1, pick2 = lane == e1, lane == e2
        picked = jnp.where(pick1 | pick2, 1.0, 0.0).astype(BF16)
        ahead = _dot(tri_ref[...], picked) + (offv_ref[0:1, :] - 1.0)
        s1 = jnp.sum(jnp.where(pick1, ahead, 0.0), axis=1, keepdims=True)
        s2 = jnp.sum(jnp.where(pick2, ahead, 0.0), axis=1, keepdims=True)
        slots_c = jnp.where(lane == 0, s1, jnp.where(lane == 1, s2, -1.0))
        dc_ref[...] = slots_c.astype(I32)
        slots_t = slots_c.T.astype(I32)
        gate_w = jnp.where(lane == 0, route[:, ROUTE_LANE + 2:ROUTE_LANE + 3],
                           jnp.where(lane == 1, route[:, ROUTE_LANE + 3:ROUTE_LANE + 4], 0.0)).T
        d1, d2 = slots_t[0:1, :], slots_t[1:2, :]
        w1, w2 = gate_w[0:1, :], gate_w[1:2, :]
        for c in range(slots // MOE_CHUNK):
            rows = slice(c * MOE_CHUNK, (c + 1) * MOE_CHUNK)
            slot = lax.broadcasted_iota(I32, (MOE_CHUNK, tm), 0) + c * MOE_CHUNK
            m1, m2 = slot == d1, slot == d2
            sel = jnp.where(m1 | m2, 1.0, 0.0).astype(BF16)
            xs_ref[rows, :] = _dot(sel, t).astype(BF16)
            g = jnp.sum(jnp.where(m1, w1, 0.0) + jnp.where(m2, w2, 0.0), axis=1, keepdims=True)
            gs_ref[rows, :] = jnp.broadcast_to(g, (MOE_CHUNK, LANES))
        xs_ref[slots:, :] = jnp.zeros((xs_ref.shape[0] - slots, xs_ref.shape[1]), BF16)
        gs_ref[slots:, :] = jnp.zeros((gs_ref.shape[0] - slots, LANES), F32)
        ys_ref[...] = jnp.zeros_like(ys_ref)

    def expert_chunk(k, start, end=None):
        rows = pl.ds(pl.multiple_of(start, MOE_ROW_ALIGN), MOE_CHUNK)
        x = xs_ref[rows, :]
        a = _dot(x, wg_ref[k])
        b = _dot(x, wu_ref[k])
        g = gs_ref[rows, :]
        hh = (a / (1.0 + jnp.exp(-a))) * b * jnp.concatenate([g] * (a.shape[1] // LANES), axis=1)
        y = _dot(hh.astype(BF16), wd_ref[k])
        if end is not None:
            slot = start + lax.broadcasted_iota(I32, y.shape, 0)
            y = jnp.where(slot < end, y, ys_ref[rows, :].astype(F32))
        ys_ref[rows, :] = y.astype(BF16)

    offs = [off_ref[i * n_exp + j * epb + k] for k in range(epb)]
    for k in range(epb):
        expert_chunk(k, offs[k])
    for k in range(epb):
        n_chunks = nch_ref[i * n_exp + j * epb + k]
        end = end_ref[i * n_exp + j * epb + k]

        def more(c, carry, k=k, end=end):
            expert_chunk(k, offs[k] + c * MOE_CHUNK, end)
            return carry

        lax.fori_loop(1, n_chunks, more, 0)

    @pl.when(j == pl.num_programs(1) - 1)
    def _():
        dc1, dc2 = dc_ref[:, 0:1], dc_ref[:, 1:2]
        acc = x1_ref[...]
        kw = 2 * LANES
        for kc in range(slots // kw):
            slot = lax.broadcasted_iota(I32, (tm, kw), 1) + kc * kw
            st = jnp.where((slot == dc1) | (slot == dc2), 1.0, 0.0).astype(BF16)
            acc = acc + _dot(st, ys_ref[kc * kw:(kc + 1) * kw, :])
        o_ref[...] = acc


def _moe_routed(t_bf, route, x1, wg, wu, wd, layer, tm, epb):
    n, d = x1.shape
    n_exp, _, dh = wg.shape[1:]
    n_tiles = n // tm
    slots = -(-(tm * TOP_K_INNER + n_exp * (MOE_ROW_ALIGN - 1)) // (2 * LANES)) * (2 * LANES)
    e12 = route[:, ROUTE_LANE:ROUTE_LANE + TOP_K_INNER].astype(I32).reshape(n_tiles, tm * TOP_K_INNER)
    cnt = jnp.sum(jax.nn.one_hot(e12, n_exp, dtype=I32), axis=1)
    cnt_al = -(-cnt // MOE_ROW_ALIGN) * MOE_ROW_ALIGN
    off = jnp.cumsum(cnt_al, axis=1) - cnt_al
    n_chunks = -(-cnt // MOE_CHUNK)
    off_v = jnp.pad(off.astype(F32)[:, None, :], ((0, 0), (0, 7), (0, LANES - n_exp)))
    tri = jnp.asarray(np.tril(np.ones((tm, tm), np.float32)), dtype=BF16)
    row = lambda i, j, *tables: (i, 0)
    tile3 = lambda i, j, *tables: (i, 0, 0)
    const = lambda i, j, *tables: (0, 0)
    wmap = lambda i, j, *tables: (layer, j, 0, 0)
    n_rows = slots + MOE_CHUNK
    grid_spec = pltpu.PrefetchScalarGridSpec(
        num_scalar_prefetch=3,
        grid=(n_tiles, n_exp // epb),
        in_specs=[pl.BlockSpec((tm, d), row), pl.BlockSpec((tm, d), row), pl.BlockSpec((tm, LANES), row),
                  pl.BlockSpec((None, 8, LANES), tile3), pl.BlockSpec((tm, tm), const),
                  pl.BlockSpec((None, epb, d, dh), wmap), pl.BlockSpec((None, epb, d, dh), wmap),
                  pl.BlockSpec((None, epb, dh, d), wmap)],
        out_specs=pl.BlockSpec((tm, d), row),
        scratch_shapes=[pltpu.VMEM((n_rows, d), BF16), pltpu.VMEM((n_rows, d), BF16),
                        pltpu.VMEM((n_rows, LANES), F32), pltpu.VMEM((tm, LANES), I32)],
    )
    return pl.pallas_call(
        functools.partial(_moe_routed_kernel, epb=epb, n_exp=n_exp, slots=slots),
        out_shape=jax.ShapeDtypeStruct((n, d), F32),
        grid_spec=grid_spec,
        compiler_params=_cparams("parallel", "arbitrary"),
        name="moe_routed",
    )(off.reshape(-1).astype(I32), (off + cnt_al).reshape(-1).astype(I32), n_chunks.reshape(-1).astype(I32),
      t_bf, x1, route, off_v, tri, wg, wu, wd)


def _block_diag_rows(w):
    g, a, b = w.shape
    eye = jnp.eye(g, dtype=w.dtype)
    return jnp.einsum('gab,gh->gahb', w, eye).reshape(g * a, g * b)


def _seg_cumsum_matrix(rows, seg):
    i = np.arange(rows)[:, None]
    j = np.arange(rows)[None, :]
    l = (j <= i) & (i // seg == j // seg)
    return jnp.asarray(np.concatenate([l, l], axis=1), dtype=BF16)


def kernel(x_prompt, x_sample, cache_k, cache_v, state_ssm_re, state_ssm_im, state_pool, page_table, norm_mix_g, w_in, q_norm_g, k_norm_g, sb_bias, ssm_lambda_re, ssm_lambda_im, ssm_log_dt, ssm_b_re, ssm_b_im, ssm_c_re, ssm_c_im, ssm_d, ssm_glu_w, ssm_glu_b, pool_w, pool_scale, w_out, norm_ffn_g, router_group_w, router_group_b, router_expert_w, router_expert_b, expert_w_gate, expert_w_up, expert_w_down):
    depth = w_in.shape[0]
    bsz, seq, d = x_prompt.shape
    db, dseq, _ = x_sample.shape
    n_phys, page, nh, hd = cache_k.shape[1:]
    assert hd == HEAD_DIM
    aw = nh * hd
    n_grp_ssm, n_state, grp_ch = ssm_b_re.shape[1:]
    sw = n_grp_ssm * grp_ch
    ns = n_grp_ssm * n_state
    pool_buf = state_pool.shape[2]
    assert pool_buf < POOL_HIST and max(POOL_WINDOWS) - 1 <= pool_buf
    n_exp_grp = router_group_w.shape[2]
    n_exp = router_expert_w.shape[2]
    assert n_exp + n_exp_grp <= ROUTE_LANE and ROUTE_LANE + 2 * TOP_K_INNER <= LANES
    past_len = page_table.shape[1] * page
    np_tok = bsz * seq
    ns_tok = db * dseq
    t_rows = 128

    w_in_bf = w_in.astype(BF16)
    w_out_bf = w_out.astype(BF16)
    wg_bf = expert_w_gate.astype(BF16)
    wu_bf = expert_w_up.astype(BF16)
    wd_bf = expert_w_down.astype(BF16)
    glu_w_bf = ssm_glu_w.astype(BF16)
    ones_bd = jnp.asarray(np.kron(np.eye(nh), np.ones((hd, hd))), dtype=BF16)
    cache_kt = cache_k.transpose(0, 1, 3, 4, 2).reshape(depth, n_phys, aw, page)
    cache_vt = cache_v.transpose(0, 1, 3, 4, 2).reshape(depth, n_phys, aw, page)
    head_eye = jnp.eye(nh, dtype=BF16)
    l2_prompt = _seg_cumsum_matrix(t_rows, t_rows)
    l2_sample = _seg_cumsum_matrix(ns_tok, dseq)
    seg_rows = POOL_HIST + dseq

    xp = x_prompt.reshape(np_tok, d)
    xs = x_sample.reshape(ns_tok, d)
    outs = {k: [] for k in ("kp", "vp", "hrp", "hip", "pbp", "ks", "vs", "hrs", "his", "pbs")}

    for l in range(depth):
        g_mix = norm_mix_g[l][None, :]
        qg_t = jnp.tile(q_norm_g[l], nh)[None, :]
        kg_t = jnp.tile(k_norm_g[l], nh)[None, :]
        bias_rows = jnp.broadcast_to(sb_bias[l][:, None], (nh, LANES))
        bias_rows_s = jnp.broadcast_to(jnp.repeat(sb_bias[l], dseq)[:, None], (nh * dseq, LANES))
        lr = ssm_lambda_re[l].reshape(1, ns)
        li = ssm_lambda_im[l].reshape(1, ns)
        ldt = jnp.broadcast_to(ssm_log_dt[l][:, None], (n_grp_ssm, n_state)).reshape(1, ns)
        bre_bd = _block_diag_rows(jnp.swapaxes(ssm_b_re[l], 1, 2))
        bim_bd = _block_diag_rows(jnp.swapaxes(ssm_b_im[l], 1, 2))
        cre_bd = _block_diag_rows(jnp.swapaxes(ssm_c_re[l], 1, 2))
        cim_bd = _block_diag_rows(jnp.swapaxes(ssm_c_im[l], 1, 2))
        prep = _ssm_prep(lr, li, ldt, bre_bd, bim_bd, cre_bd, cim_bd, t_rows)
        prep_s = tuple(prep[:3]) + tuple(jnp.tile(tb[:dseq], (db, 1)) for tb in prep[3:])
        dskip = ssm_d[l][None, :]
        glub = ssm_glu_b[l][None, :]
        pw_bd = _block_diag_rows(pool_w[l]).astype(BF16)
        pscale = pool_scale[l][None, :]
        g_ffn = norm_ffn_g[l][None, :]
        lane_pad = ((0, 0), (0, LANES - n_exp - n_exp_grp))
        rw = jnp.pad(jnp.concatenate([router_expert_w[l], router_group_w[l]], axis=1), lane_pad)
        rw_hi = rw.astype(BF16)
        rw_lo = (rw - rw_hi.astype(F32)).astype(BF16)
        rb = jnp.pad(jnp.concatenate([router_expert_b[l], router_group_b[l]])[None, :], lane_pad)

        q_bf, k32, v32, k_bf, v_bf, us, up = _inproj(xp, g_mix, w_in_bf[l], qg_t, kg_t, ones_bd, tm=512)
        attn = _attn_prompt(q_bf, k_bf, v_bf, bias_rows, bsz, seq, tq=1024)
        sso, poo, hlast = _mixers_prompt(us, up, prep, l2_prompt, dskip, glu_w_bf[l], glub, pw_bd, pscale,
                                         bsz, seq, t_rows)
        x1, t_bf, gates = _outproj(xp, attn, sso, poo, w_out_bf[l], g_ffn, rw_hi, rw_lo, rb, n_exp, n_exp_grp, tm=512)
        xp = _moe_routed(t_bf, gates, x1, wg_bf, wu_bf, wd_bf, l, tm=1024, epb=4)
        outs["kp"].append(k32.reshape(np_tok // page, page, nh, hd))
        outs["vp"].append(v32.reshape(np_tok // page, page, nh, hd))
        outs["hrp"].append(hlast[:, 0, :ns].reshape(bsz, n_grp_ssm, n_state))
        outs["hip"].append(hlast[:, 0, ns:].reshape(bsz, n_grp_ssm, n_state))
        outs["pbp"].append(up.reshape(bsz, seq, sw)[:, seq - pool_buf:, :])

        q_bf, k32, v32, k_bf, v_bf, us, up = _inproj(xs, g_mix, w_in_bf[l], qg_t, kg_t, ones_bd, tm=ns_tok)
        qbd = jnp.einsum('bthd,hg->bhtgd', q_bf.reshape(db, dseq, nh, hd), head_eye).reshape(db, nh * dseq, aw)
        pad_rows = ((0, 0), (0, LANES - dseq), (0, 0))
        attn = _attn_sample(qbd, jnp.pad(k32.reshape(db, dseq, aw), pad_rows), jnp.pad(v32.reshape(db, dseq, aw), pad_rows),
                            cache_kt, cache_vt, page_table, bias_rows_s, l, n_pages_step=16, nq=dseq)
        ext = jnp.concatenate([jnp.zeros((db, POOL_HIST - pool_buf, sw), F32), state_pool[l],
                               up.reshape(db, dseq, sw)], axis=1)
        h0 = jnp.concatenate([state_ssm_re[l].reshape(db, ns), state_ssm_im[l].reshape(db, ns)], axis=1)
        h0rep = jnp.repeat(h0, dseq, axis=0)
        sso, poo_ext, hall = _mixers_sample(us, ext.reshape(db * seg_rows, sw), h0rep, prep_s, l2_sample, dskip,
                                            glu_w_bf[l], glub, pw_bd, pscale, seg_rows, past_len)
        poo = poo_ext.reshape(db, seg_rows, sw)[:, POOL_HIST:, :].reshape(ns_tok, sw)
        x1, t_bf, gates = _outproj(xs, attn.reshape(ns_tok, aw), sso, poo, w_out_bf[l], g_ffn, rw_hi, rw_lo, rb,
                                   n_exp, n_exp_grp, tm=ns_tok)
        xs = _moe(t_bf, gates, x1, wg_bf, wu_bf, wd_bf, l, tm=ns_tok, epb=2)
        hl = hall.reshape(db, dseq, 2 * ns)[:, dseq - 1, :]
        outs["ks"].append(k32.reshape(db, dseq, nh, hd))
        outs["vs"].append(v32.reshape(db, dseq, nh, hd))
        outs["hrs"].append(hl[:, :ns].reshape(db, n_grp_ssm, n_state))
        outs["his"].append(hl[:, ns:].reshape(db, n_grp_ssm, n_state))
        outs["pbs"].append(ext[:, seg_rows - pool_buf:, :])

    st = {k: jnp.stack(v) for k, v in outs.items()}
    return (xp.reshape(bsz, seq, d), xs.reshape(db, dseq, d), st["kp"], st["vp"], st["hrp"], st["hip"], st["pbp"],
            st["ks"], st["vs"], st["hrs"], st["his"], st["pbs"])
```

```python
import functools
import math

import numpy as np
import jax
import jax.numpy as jnp
from jax import lax
from jax.experimental import pallas as pl
from jax.experimental.pallas import tpu as pltpu

F32 = jnp.float32
BF16 = jnp.bfloat16
I32 = jnp.int32

EPS = 1e-6
HEAD_DIM = 64
POOL_WINDOWS = (2, 4, 8, 16)
POOL_HIST = 16
TOP_K_INNER = 2
LANES = 128
VMEM_LIMIT = 56 * 1024 * 1024

NEG_INF = float("-inf")
LOG2E = math.log2(math.e)
ROUTE_LANE = 64
MOE_ROW_ALIGN = 16
MOE_CHUNK = 128


def _cparams(*sem):
    return pltpu.CompilerParams(dimension_semantics=sem, vmem_limit_bytes=VMEM_LIMIT)


def _split_bf16(x):
    hi = x.astype(BF16)
    lo = (x - hi.astype(F32)).astype(BF16)
    return hi, lo


def _dot(a, b):
    return jnp.dot(a, b, preferred_element_type=F32)


def _dot_nt(a, b):
    return lax.dot_general(a, b, (((1,), (1,)), ((), ())), preferred_element_type=F32)


def _dot_tn(a, b):
    return lax.dot_general(a, b, (((0,), (0,)), ((), ())), preferred_element_type=F32)


def _neg_abs(z):
    bits = lax.bitcast_convert_type(z, jnp.uint32) | jnp.uint32(0x80000000)
    return lax.bitcast_convert_type(bits, F32)


def _neg_softplus2(z):
    return -(jnp.maximum(z, 0.0) + jnp.log2(1.0 + jnp.exp2(_neg_abs(z))))


def _inproj_kernel(x_ref, g_ref, w_ref, qg_ref, kg_ref, ones_ref,
                   q_ref, k_ref, v_ref, kb_ref, vb_ref, us_ref, up_ref, *, aw, sw, kv_pages):
    x = x_ref[...]
    ms = jnp.mean(x * x, axis=-1, keepdims=True)
    h = (x * lax.rsqrt(ms + EPS) * g_ref[...]).astype(BF16)
    proj = _dot(h, w_ref[...])
    ones_bd = ones_ref[...]

    def headnorm(t, gain):
        hi, lo = _split_bf16(t * t)
        ss = _dot(hi, ones_bd) + _dot(lo, ones_bd)
        return t * lax.rsqrt(ss * (1.0 / HEAD_DIM) + EPS) * gain

    q = headnorm(proj[:, :aw], qg_ref[...])
    k = headnorm(proj[:, aw:2 * aw], kg_ref[...])
    v = proj[:, 2 * aw:3 * aw]
    q_ref[...] = (q * (LOG2E * HEAD_DIM ** -0.5)).astype(BF16)
    if kv_pages:
        for p in range(k_ref.shape[0]):
            k_ref[p] = k[p * LANES:(p + 1) * LANES, :].T
            v_ref[p] = v[p * LANES:(p + 1) * LANES, :].T
    else:
        k_ref[...] = k
        v_ref[...] = v
    kb_ref[...] = k.astype(BF16)
    vb_ref[...] = v.astype(BF16)
    us_ref[...] = proj[:, 3 * aw:3 * aw + sw]
    up_ref[...] = proj[:, 3 * aw + sw:]


def _inproj(x2, g_mix, w_in_bf, qg_t, kg_t, ones_bd, tm, kv_pages=False):
    n, d = x2.shape
    aw = ones_bd.shape[0]
    sw = (w_in_bf.shape[1] - 3 * aw) // 2
    row = lambda i: (i, 0)
    const = lambda i: (0, 0)
    sds = jax.ShapeDtypeStruct
    if kv_pages:
        kv_shape = sds((n // LANES, aw, LANES), F32)
        kv_spec = pl.BlockSpec((tm // LANES, aw, LANES), lambda i: (i, 0, 0))
    else:
        kv_shape = sds((n, aw), F32)
        kv_spec = pl.BlockSpec((tm, aw), row)
    return pl.pallas_call(
        functools.partial(_inproj_kernel, aw=aw, sw=sw, kv_pages=kv_pages),
        out_shape=[sds((n, aw), BF16), kv_shape, kv_shape,
                   sds((n, aw), BF16), sds((n, aw), BF16), sds((n, sw), F32), sds((n, sw), F32)],
        grid=(n // tm,),
        in_specs=[pl.BlockSpec((tm, d), row), pl.BlockSpec((1, d), const),
                  pl.BlockSpec(w_in_bf.shape, const), pl.BlockSpec((1, aw), const),
                  pl.BlockSpec((1, aw), const), pl.BlockSpec((aw, aw), const)],
        out_specs=[pl.BlockSpec((tm, aw), row), kv_spec, kv_spec]
        + [pl.BlockSpec((tm, aw), row)] * 2 + [pl.BlockSpec((tm, sw), row)] * 2,
        compiler_params=_cparams("parallel"),
        name="inproj",
    )(x2, g_mix, w_in_bf, qg_t, kg_t, ones_bd)


def _attn_prompt_kernel(qi_ref, kj_ref, q_ref, k_ref, v_ref, uu_ref, bias_ref, o_ref,
                        r_ref, acc_ref, *, tq, tkb, sub, rc):
    s = pl.program_id(2)
    hp = pl.program_id(1)
    qi = qi_ref[s]
    kj = kj_ref[s]
    is_diag = kj == qi

    @pl.when(is_diag)
    def _():
        r_ref[...] = jnp.zeros_like(r_ref)
        acc_ref[...] = jnp.zeros_like(acc_ref)

    n_sub = tkb // sub
    n_rc = tq // rc
    lane_k = lax.broadcasted_iota(I32, (sub, LANES), 1)
    d_idx = lax.broadcasted_iota(I32, (rc, sub), 1) - lax.broadcasted_iota(I32, (rc, sub), 0)
    uu = uu_ref[...]
    zero_bf = jnp.zeros((), BF16)
    head_lanes = (lane_k < HEAD_DIM, lane_k >= HEAD_DIM)

    def offset_lanes(hd):
        b = bias_ref[pl.ds(2 * hp + hd, 1), :] * LOG2E
        b0 = b.astype(BF16)
        b1 = (b - b0.astype(F32)).astype(BF16)
        b2 = (b - b0.astype(F32) - b1.astype(F32)).astype(BF16)
        row = jnp.where(lane_k[0:1, :] == 0, b0, jnp.where(lane_k[0:1, :] == 1, b1,
                                                           jnp.where(lane_k[0:1, :] == 2, b2, zero_bf)))
        return jnp.broadcast_to(row, (sub, LANES))

    k_offset = jnp.concatenate([offset_lanes(0), offset_lanes(1)], axis=0)
    q_ones = jnp.where(lax.broadcasted_iota(I32, (rc, LANES), 1) < 3, 1.0, 0.0).astype(BF16)

    def run(tiles, masked):
        kv = {}

        def kv_cat(c):
            if c not in kv:
                kc = k_ref[pl.ds(c * sub, sub), :]
                vc = v_ref[pl.ds(c * sub, sub), :]
                kv[c] = (jnp.concatenate([jnp.where(m, kc, zero_bf) for m in head_lanes], axis=0),
                         jnp.concatenate([jnp.where(m, vc, zero_bf) for m in head_lanes], axis=0))
            return kv[c]

        def scores(c, r):
            q_ext = jnp.concatenate([q_ref[pl.ds(r * rc, rc), :], q_ones], axis=1)
            k_ext = jnp.concatenate([kv_cat(c)[0], k_offset], axis=1)
            return _dot_nt(q_ext, k_ext)

        def suffix_sums(c, r, z2):
            valid = d_idx < (r * rc - c * sub) if masked else None
            zs, cums = [], []
            for hd in range(2):
                z = z2[:, hd * sub:(hd + 1) * sub]
                sp = jnp.maximum(z, 0.0) + jnp.log2(1.0 + jnp.exp2(_neg_abs(z)))
                if masked:
                    sp = jnp.where(valid, sp, 0.0)
                zs.append(z)
                cums.append(_dot(sp.astype(BF16), uu[:sub]))
            return zs, cums, valid

        def weights_and_values(c, r, zs, cums, valid):
            rows = pl.ds(r * rc, rc)
            ps = []
            for hd in range(2):
                rr = r_ref[hd, rows, :]
                p = jnp.exp2(zs[hd] - (cums[hd] + rr))
                if masked:
                    p = jnp.where(valid, p, 0.0)
                r_ref[hd, rows, :] = rr + jnp.broadcast_to(cums[hd][:, 0:1], rr.shape)
                ps.append(p.astype(BF16))
            acc_ref[rows, :] += _dot(jnp.concatenate(ps, axis=1), kv_cat(c)[1])

        z2 = scores(*tiles[0])
        for n, (c, r) in enumerate(tiles):
            mid = suffix_sums(c, r, z2)
            if n + 1 < len(tiles):
                z2 = scores(*tiles[n + 1])
            weights_and_values(c, r, *mid)

    @pl.when(is_diag)
    def _():
        run([(c, r) for c in range(n_sub - 1, -1, -1) for r in range((c * sub) // rc, n_rc)], True)

    @pl.when(jnp.logical_not(is_diag))
    def _():
        run([(c, r) for c in range(n_sub - 1, -1, -1) for r in range(n_rc)], False)

    @pl.when(kj == 0)
    def _():
        o_ref[...] = acc_ref[...].astype(o_ref.dtype)


def _suffix_matrix(sub):
    j = np.arange(2 * sub)[:, None] % sub
    s = np.arange(sub)[None, :]
    return jnp.asarray(j >= s, dtype=BF16)


def _attn_prompt(q_bf, k_bf, v_bf, bias_rows, bsz, seq, tq):
    n, aw = q_bf.shape
    tkb = tq
    sub = LANES
    rc = 512
    nq = seq // tq
    qi_tbl = np.concatenate([np.full(i + 1, i) for i in range(nq)]).astype(np.int32)
    kj_tbl = np.concatenate([np.arange(i, -1, -1) for i in range(nq)]).astype(np.int32)
    n_steps = qi_tbl.shape[0]
    n_hp = aw // LANES
    uu = _suffix_matrix(sub)
    grid_spec = pltpu.PrefetchScalarGridSpec(
        num_scalar_prefetch=2,
        grid=(bsz, n_hp, n_steps),
        in_specs=[
            pl.BlockSpec((tq, LANES), lambda b, h, s, qi, kj: (b * nq + qi[s], h)),
            pl.BlockSpec((tkb, LANES), lambda b, h, s, qi, kj: (b * nq + kj[s], h)),
            pl.BlockSpec((tkb, LANES), lambda b, h, s, qi, kj: (b * nq + kj[s], h)),
            pl.BlockSpec(uu.shape, lambda b, h, s, qi, kj: (0, 0)),
            pl.BlockSpec(bias_rows.shape, lambda b, h, s, qi, kj: (0, 0)),
        ],
        out_specs=pl.BlockSpec((tq, LANES), lambda b, h, s, qi, kj: (b * nq + qi[s], h)),
        scratch_shapes=[pltpu.VMEM((2, tq, LANES), F32), pltpu.VMEM((tq, LANES), F32)],
    )
    return pl.pallas_call(
        functools.partial(_attn_prompt_kernel, tq=tq, tkb=tkb, sub=sub, rc=rc),
        out_shape=jax.ShapeDtypeStruct((n, aw), BF16),
        grid_spec=grid_spec,
        compiler_params=_cparams("parallel", "parallel", "arbitrary"),
        name="attn_prompt",
    )(jnp.asarray(qi_tbl), jnp.asarray(kj_tbl), q_bf, k_bf, v_bf, uu, bias_rows)


def _attn_sample_kernel(pt_ref, q_ref, kn_ref, vn_ref, bias_ref, uu_ref, *rest, n_pages_step, nq, nh):
    k_refs = rest[:n_pages_step]
    v_refs = rest[n_pages_step:2 * n_pages_step]
    o_ref = rest[2 * n_pages_step]
    r_ref, acc_ref = rest[2 * n_pages_step + 1:]
    del pt_ref
    j = pl.program_id(1)
    nrow = nh * nq
    bias = bias_ref[...] * LOG2E
    qbd = q_ref[...]
    uu = uu_ref[...]

    def softplus2(z):
        return jnp.maximum(z, 0.0) + jnp.log2(1.0 + jnp.exp2(_neg_abs(z)))

    @pl.when(j == 0)
    def _():
        z = _dot_nt(qbd, kn_ref[...].astype(BF16)) + bias
        lane_tok = lax.broadcasted_iota(I32, (nrow, LANES), 1)
        row_q = lax.broadcasted_iota(I32, (nrow, LANES), 0) % nq
        valid = lane_tok < row_q
        hi, lo = _split_bf16(jnp.where(valid, softplus2(z), 0.0))
        ct = _dot(jnp.concatenate([hi, lo], axis=1), uu)
        p = jnp.where(valid, jnp.exp2(z - ct[:, :LANES]), 0.0)
        r_ref[...] = ct[:, LANES:]
        acc_ref[...] = _dot(p.astype(BF16), vn_ref[...].astype(BF16))

    pages = range(n_pages_step - 1, -1, -1)
    zz = {i: _dot(qbd, k_refs[i][...].astype(BF16)) + bias for i in pages}
    hi, lo = _split_bf16(jnp.concatenate([softplus2(zz[i]) for i in pages], axis=0))
    ct = _dot(jnp.concatenate([hi, lo], axis=1), uu)
    rr = r_ref[...]
    ps = {}
    for n, i in enumerate(pages):
        ps[i] = jnp.exp2(zz[i] - (ct[n * nrow:(n + 1) * nrow, :LANES] + rr)).astype(BF16)
        rr = rr + ct[n * nrow:(n + 1) * nrow, LANES:]
    acc = acc_ref[...]
    for i in pages:
        acc = acc + _dot_nt(ps[i], v_refs[i][...].astype(BF16))
    r_ref[...] = rr
    acc_ref[...] = acc

    @pl.when(j == pl.num_programs(1) - 1)
    def _():
        row_head = lax.broadcasted_iota(I32, acc.shape, 0) // nq
        col_head = lax.broadcasted_iota(I32, acc.shape, 1) // HEAD_DIM
        own = jnp.where(row_head == col_head, acc, 0.0)
        out = own[0:nq, :]
        for h in range(1, nh):
            out = out + own[h * nq:(h + 1) * nq, :]
        o_ref[...] = out.astype(o_ref.dtype)


def _attn_sample(qbd, kn_pad, vn_pad, cache_kt, cache_vt, page_table, bias_rows, layer, n_pages_step, nq):
    db, nrow, aw = qbd.shape
    nh = nrow // nq
    page = cache_kt.shape[3]
    assert page == LANES
    n_pages = page_table.shape[1]
    n_steps = n_pages // n_pages_step
    jj = np.arange(2 * LANES)[:, None] % LANES
    ss = np.arange(2 * LANES)[None, :]
    uu = jnp.asarray((ss >= LANES) | (jj >= ss), dtype=BF16)

    def page_map(i):
        return lambda b, j, pt: (layer, pt[b, (n_steps - 1 - j) * n_pages_step + i], 0, 0)

    seq_map = lambda b, j, pt: (b, 0, 0)
    const = lambda b, j, pt: (0, 0)
    page_spec = [pl.BlockSpec((None, None, aw, page), page_map(i)) for i in range(n_pages_step)]
    grid_spec = pltpu.PrefetchScalarGridSpec(
        num_scalar_prefetch=1,
        grid=(db, n_steps),
        in_specs=[pl.BlockSpec((None, nrow, aw), seq_map), pl.BlockSpec((None, LANES, aw), seq_map),
                  pl.BlockSpec((None, LANES, aw), seq_map), pl.BlockSpec(bias_rows.shape, const),
                  pl.BlockSpec(uu.shape, const)] + page_spec + page_spec,
        out_specs=pl.BlockSpec((None, nq, aw), seq_map),
        scratch_shapes=[pltpu.VMEM((nrow, LANES), F32), pltpu.VMEM((nrow, aw), F32)],
    )
    return pl.pallas_call(
        functools.partial(_attn_sample_kernel, n_pages_step=n_pages_step, nq=nq, nh=nh),
        out_shape=jax.ShapeDtypeStruct((db, nq, aw), BF16),
        grid_spec=grid_spec,
        compiler_params=_cparams("parallel", "arbitrary"),
        name="attn_sample",
    )(page_table, qbd, kn_pad, vn_pad, bias_rows, uu, *([cache_kt] * n_pages_step), *([cache_vt] * n_pages_step))


def _ssm_prep_kernel(lr_ref, li_ref, ldt_ref, bre_ref, bim_ref, cre_ref, cim_ref,
                     a_ref, bbar_ref, cbd_ref, pre_ref, pim_ref, qre_ref, qim_ref, *, t_rows):
    lr = lr_ref[...]
    li = li_ref[...]
    dt = jnp.exp(ldt_ref[...])
    mag = jnp.exp(lr * dt)
    ar = mag * jnp.cos(li * dt)
    ai = mag * jnp.sin(li * dt)
    er = ar - 1.0
    den = lr * lr + li * li
    fr = (er * lr + ai * li) / den
    fi = (ai * lr - er * li) / den
    a_ref[0:1, :] = ar
    a_ref[1:2, :] = ai
    bre = bre_ref[...]
    bim = bim_ref[...]
    ns = lr.shape[1]
    bbar_ref[:, :ns] = (fr * bre - fi * bim).astype(BF16)
    bbar_ref[:, ns:] = (fr * bim + fi * bre).astype(BF16)
    cbd_ref[:ns, :] = cre_ref[...].astype(BF16)
    cbd_ref[ns:, :] = (-cim_ref[...]).astype(BF16)
    t = lax.broadcasted_iota(I32, (t_rows, ns), 0).astype(F32)
    grow = jnp.exp(t * (lr * dt))
    shrink = jnp.exp(-t * (lr * dt))
    ang = t * (li * dt)
    c = jnp.cos(ang)
    s = jnp.sin(ang)
    pre_ref[...] = grow * c
    pim_ref[...] = grow * s
    qre_ref[...] = shrink * c
    qim_ref[...] = -(shrink * s)


def _ssm_prep(lr, li, ldt, bre_bd, bim_bd, cre_bd, cim_bd, t_rows):
    ns = lr.shape[1]
    sw = bre_bd.shape[0]
    sds = jax.ShapeDtypeStruct
    return pl.pallas_call(
        functools.partial(_ssm_prep_kernel, t_rows=t_rows),
        out_shape=[sds((2, ns), F32), sds((sw, 2 * ns), BF16), sds((2 * ns, sw), BF16)]
        + [sds((t_rows, ns), F32)] * 4,
        compiler_params=pltpu.CompilerParams(vmem_limit_bytes=VMEM_LIMIT),
        name="ssm_prep",
    )(lr, li, ldt, bre_bd, bim_bd, cre_bd, cim_bd)


def _ssm_compute(u32, bbar, cbd, a, pre, pim, qre, qim, l2, hp_re, hp_im, dskip, gluw, glub):
    ns = pre.shape[1]
    bu = _dot(u32.astype(BF16), bbar)
    bur, bui = bu[:, :ns], bu[:, ns:]
    xr = qre * bur - qim * bui
    xi = qre * bui + qim * bur
    x_hi, x_lo = _split_bf16(jnp.concatenate([xr, xi], axis=1))
    s = _dot(l2, jnp.concatenate([x_hi, x_lo], axis=0))
    ar, ai = a[0:1, :], a[1:2, :]
    sr = s[:, :ns] + (ar * hp_re - ai * hp_im)
    si = s[:, ns:] + (ar * hp_im + ai * hp_re)
    hr = pre * sr - pim * si
    hi_ = pre * si + pim * sr
    hcat = jnp.concatenate([hr, hi_], axis=1)
    y = _dot(hcat.astype(BF16), cbd) + dskip * u32
    g = 0.5 * y * (1.0 + jnp.tanh(math.sqrt(2.0 / math.pi) * (y + 0.044715 * (y * y * y))))
    gate = 1.0 / (1.0 + jnp.exp(-(_dot(g.astype(BF16), gluw) + glub)))
    return g * gate, hcat


def _pool_compute(ext, pos, pw_bd, pscale):
    s2 = ext + pltpu.roll(ext, 1, 0)
    s4 = s2 + pltpu.roll(s2, 2, 0)
    s8 = s4 + pltpu.roll(s4, 4, 0)
    s16 = s8 + pltpu.roll(s8, 8, 0)
    grp = lax.broadcasted_iota(I32, ext.shape, 1) // (ext.shape[1] // len(POOL_WINDOWS))
    win = jnp.where(grp == 0, s2, jnp.where(grp == 1, s4, jnp.where(grp == 2, s8, s16)))
    w = jnp.where(grp == 0, POOL_WINDOWS[0],
                  jnp.where(grp == 1, POOL_WINDOWS[1], jnp.where(grp == 2, POOL_WINDOWS[2], POOL_WINDOWS[3])))
    cnt = jnp.maximum(jnp.minimum(pos + 1, w), 1).astype(F32)
    d = win / cnt - ext
    return _dot(d.astype(BF16), pw_bd) * pscale


def _mixers_prompt_kernel(us_ref, up_ref, bbar_ref, cbd_ref, a_ref, pre_ref, pim_ref, qre_ref, qim_ref, l2_ref,
                          dskip_ref, gluw_ref, glub_ref, pw_ref, ps_ref,
                          so_ref, po_ref, hl_ref, carry_ref, ext_ref, *, t_rows):
    c = pl.program_id(0)
    ns = pre_ref.shape[1]
    bsz = us_ref.shape[0]

    @pl.when(c == 0)
    def _():
        carry_ref[...] = jnp.zeros_like(carry_ref)
        ext_ref[:, 0:POOL_HIST, :] = jnp.zeros((bsz, POOL_HIST, ext_ref.shape[2]), F32)

    pos = c * t_rows - POOL_HIST + lax.broadcasted_iota(I32, (t_rows + POOL_HIST, 1), 0)
    for b in range(bsz):
        carry = carry_ref[b]
        out, hcat = _ssm_compute(us_ref[b], bbar_ref[...], cbd_ref[...], a_ref[...], pre_ref[...], pim_ref[...],
                                 qre_ref[...], qim_ref[...], l2_ref[...], carry[:, :ns], carry[:, ns:],
                                 dskip_ref[...], gluw_ref[...], glub_ref[...])
        so_ref[b] = out.astype(so_ref.dtype)
        last = hcat[t_rows - 1:t_rows, :]
        carry_ref[b] = last
        hl_ref[b] = last

        ext_ref[b, POOL_HIST:, :] = up_ref[b]
        ext = ext_ref[b]
        pooled = _pool_compute(ext, pos, pw_ref[...], ps_ref[...])
        po_ref[b] = pooled[POOL_HIST:, :].astype(po_ref.dtype)
        ext_ref[b, 0:POOL_HIST, :] = ext[t_rows:, :]


def _mixers_prompt(us, up, prep, l2, dskip, gluw, glub, pw_bd, pscale, bsz, seq, t_rows):
    a, bbar, cbd, pre, pim, qre, qim = prep
    n, sw = us.shape
    ns = pre.shape[1]
    nc = seq // t_rows
    chunk = lambda c: (0, c, 0)
    full = lambda arr: pl.BlockSpec(arr.shape, lambda c: (0,) * arr.ndim)
    sds = jax.ShapeDtypeStruct
    so, po, hl = pl.pallas_call(
        functools.partial(_mixers_prompt_kernel, t_rows=t_rows),
        out_shape=[sds((bsz, seq, sw), BF16), sds((bsz, seq, sw), BF16), sds((bsz, 1, 2 * ns), F32)],
        grid=(nc,),
        in_specs=[pl.BlockSpec((bsz, t_rows, sw), chunk), pl.BlockSpec((bsz, t_rows, sw), chunk),
                  full(bbar), full(cbd), full(a), full(pre), full(pim), full(qre), full(qim), full(l2),
                  full(dskip), full(gluw), full(glub), full(pw_bd), full(pscale)],
        out_specs=[pl.BlockSpec((bsz, t_rows, sw), chunk), pl.BlockSpec((bsz, t_rows, sw), chunk),
                   pl.BlockSpec((bsz, 1, 2 * ns), lambda c: (0, 0, 0))],
        scratch_shapes=[pltpu.VMEM((bsz, 1, 2 * ns), F32), pltpu.VMEM((bsz, t_rows + POOL_HIST, sw), F32)],
        compiler_params=_cparams("arbitrary"),
        name="mixers_prompt",
    )(us.reshape(bsz, seq, sw), up.reshape(bsz, seq, sw), bbar, cbd, a, pre, pim, qre, qim, l2,
      dskip, gluw, glub, pw_bd, pscale)
    return so.reshape(n, sw), po.reshape(n, sw), hl


def _mixers_sample_kernel(us_ref, ext_ref, h0_ref, bbar_ref, cbd_ref, a_ref, pre_ref, pim_ref, qre_ref, qim_ref,
                          l2_ref, dskip_ref, gluw_ref, glub_ref, pw_ref, ps_ref,
                          so_ref, po_ref, h_ref, *, seg_rows, start_pos):
    ns = pre_ref.shape[1]
    h0 = h0_ref[...]
    out, hcat = _ssm_compute(us_ref[...], bbar_ref[...], cbd_ref[...], a_ref[...], pre_ref[...], pim_ref[...],
                             qre_ref[...], qim_ref[...], l2_ref[...], h0[:, :ns], h0[:, ns:],
                             dskip_ref[...], gluw_ref[...], glub_ref[...])
    so_ref[...] = out.astype(so_ref.dtype)
    h_ref[...] = hcat
    ext = ext_ref[...]
    pos = start_pos - POOL_HIST + lax.broadcasted_iota(I32, (ext.shape[0], 1), 0) % seg_rows
    po_ref[...] = _pool_compute(ext, pos, pw_ref[...], ps_ref[...]).astype(po_ref.dtype)


def _mixers_sample(us, ext, h0rep, prep_s, l2, dskip, gluw, glub, pw_bd, pscale, seg_rows, start_pos):
    a, bbar, cbd, pre, pim, qre, qim = prep_s
    n, sw = us.shape
    ns = pre.shape[1]
    sds = jax.ShapeDtypeStruct
    return pl.pallas_call(
        functools.partial(_mixers_sample_kernel, seg_rows=seg_rows, start_pos=start_pos),
        out_shape=[sds((n, sw), BF16), sds(ext.shape, BF16), sds((n, 2 * ns), F32)],
        compiler_params=pltpu.CompilerParams(vmem_limit_bytes=VMEM_LIMIT),
        name="mixers_sample",
    )(us, ext, h0rep, bbar, cbd, a, pre, pim, qre, qim, l2, dskip, gluw, glub, pw_bd, pscale)


def _split_weights_kernel(w_ref, hi_ref, lo_ref):
    hi, lo = _split_bf16(w_ref[...])
    hi_ref[...] = hi
    lo_ref[...] = lo


def _split_weights(w):
    sds = jax.ShapeDtypeStruct
    return pl.pallas_call(
        _split_weights_kernel,
        out_shape=[sds(w.shape, BF16), sds(w.shape, BF16)],
        compiler_params=pltpu.CompilerParams(vmem_limit_bytes=VMEM_LIMIT),
        name="split_weights",
    )(w)


def _outproj_kernel(x_ref, at_ref, so_ref, po_ref, woh_ref, wol_ref, g_ref, rwh_ref, rwl_ref, rb_ref,
                    x1_ref, t_ref, gates_ref, *, n_exp, n_grp):
    mix = jnp.concatenate([at_ref[...], so_ref[...], po_ref[...]], axis=1)
    x1 = x_ref[...] + (_dot(mix, woh_ref[...]) + _dot(mix, wol_ref[...]))
    x1_ref[...] = x1
    ms = jnp.mean(x1 * x1, axis=-1, keepdims=True)
    t = x1 * lax.rsqrt(ms + EPS) * g_ref[...]
    t_ref[...] = t.astype(BF16)
    t_hi, t_lo = _split_bf16(t)
    rwh = rwh_ref[...]
    logits = _dot(t_hi, rwh) + _dot(t_hi, rwl_ref[...]) + _dot(t_lo, rwh) + rb_ref[...]
    lane = lax.broadcasted_iota(I32, logits.shape, 1)
    big = jnp.int32(2 ** 30)
    epg = n_exp // n_grp
    is_g = (lane >= n_exp) & (lane < n_exp + n_grp)
    gl = jnp.where(is_g, logits, NEG_INF)
    gmax = jnp.max(gl, axis=1, keepdims=True)
    gidx = jnp.min(jnp.where(gl == gmax, lane - n_exp, big), axis=1, keepdims=True)
    g_w = 1.0 / jnp.sum(jnp.where(is_g, jnp.exp(gl - gmax), 0.0), axis=1, keepdims=True)
    in_grp = (lane < n_exp) & (lane // epg == gidx)
    el = jnp.where(in_grp, logits, NEG_INF)
    m1 = jnp.max(el, axis=1, keepdims=True)
    i1 = jnp.min(jnp.where(el == m1, lane, big), axis=1, keepdims=True)
    el2 = jnp.where(lane == i1, NEG_INF, el)
    m2 = jnp.max(el2, axis=1, keepdims=True)
    i2 = jnp.min(jnp.where(el2 == m2, lane, big), axis=1, keepdims=True)
    e21 = jnp.exp(m2 - m1)
    w1 = 1.0 / (1.0 + e21)
    w2 = e21 * w1
    gates = jnp.where(lane == i1, w1 * g_w, jnp.where(lane == i2, w2 * g_w, 0.0))
    route = jnp.where(lane == ROUTE_LANE, i1.astype(F32),
                      jnp.where(lane == ROUTE_LANE + 1, i2.astype(F32),
                                jnp.where(lane == ROUTE_LANE + 2, w1 * g_w,
                                          jnp.where(lane == ROUTE_LANE + 3, w2 * g_w, gates))))
    gates_ref[...] = route


def _outproj(x2, attn, sso, poo, wo_hi, wo_lo, g_ffn, rw_hi, rw_lo, rb, n_exp, n_grp, tm):
    n, d = x2.shape
    aw, sw = attn.shape[1], sso.shape[1]
    row = lambda i: (i, 0)
    const = lambda i: (0, 0)
    sds = jax.ShapeDtypeStruct
    return pl.pallas_call(
        functools.partial(_outproj_kernel, n_exp=n_exp, n_grp=n_grp),
        out_shape=[sds((n, d), F32), sds((n, d), BF16), sds((n, LANES), F32)],
        grid=(n // tm,),
        in_specs=[pl.BlockSpec((tm, d), row), pl.BlockSpec((tm, aw), row), pl.BlockSpec((tm, sw), row),
                  pl.BlockSpec((tm, sw), row), pl.BlockSpec(wo_hi.shape, const), pl.BlockSpec(wo_lo.shape, const),
                  pl.BlockSpec((1, d), const),
                  pl.BlockSpec(rw_hi.shape, const), pl.BlockSpec(rw_lo.shape, const), pl.BlockSpec((1, LANES), const)],
        out_specs=[pl.BlockSpec((tm, d), row), pl.BlockSpec((tm, d), row), pl.BlockSpec((tm, LANES), row)],
        compiler_params=_cparams("parallel"),
        name="outproj",
    )(x2, attn, sso, poo, wo_hi, wo_lo, g_ffn, rw_hi, rw_lo, rb)


def _moe_kernel(t_ref, gates_ref, x1_ref, wg_ref, wu_ref, wd_ref, o_ref, *, epb):
    j = pl.program_id(1)

    @pl.when(j == 0)
    def _():
        o_ref[...] = x1_ref[...]

    t = t_ref[...]
    g_hi, g_lo = _split_bf16(gates_ref[...])
    g2 = jnp.concatenate([g_hi, g_lo], axis=1)
    dh = wg_ref.shape[2]
    sel_row = lax.broadcasted_iota(I32, (2 * LANES, dh), 0) % LANES
    acc = jnp.zeros(o_ref.shape, F32)
    for i in range(epb):
        e = j * epb + i
        gb = _dot(g2, (sel_row == e).astype(BF16))
        a = _dot(t, wg_ref[i])
        b = _dot(t, wu_ref[i])
        hh = (a / (1.0 + jnp.exp(-a))) * b * gb
        acc = acc + _dot(hh.astype(BF16), wd_ref[i])
    o_ref[...] += acc


def _moe(t_bf, gates, x1, wg, wu, wd, layer, tm, epb):
    n, d = x1.shape
    n_exp, _, dh = wg.shape[1:]
    row = lambda i, j: (i, 0)
    return pl.pallas_call(
        functools.partial(_moe_kernel, epb=epb),
        out_shape=jax.ShapeDtypeStruct((n, d), F32),
        grid=(n // tm, n_exp // epb),
        in_specs=[pl.BlockSpec((tm, d), row), pl.BlockSpec((tm, LANES), row), pl.BlockSpec((tm, d), row),
                  pl.BlockSpec((None, epb, d, dh), lambda i, j: (layer, j, 0, 0)),
                  pl.BlockSpec((None, epb, d, dh), lambda i, j: (layer, j, 0, 0)),
                  pl.BlockSpec((None, epb, dh, d), lambda i, j: (layer, j, 0, 0))],
        out_specs=pl.BlockSpec((tm, d), row),
        compiler_params=_cparams("parallel", "arbitrary"),
        name="moe",
    )(t_bf, gates, x1, wg, wu, wd)


def _moe_routed_kernel(off_ref, end_ref, nch_ref, t_ref, x1_ref, route_ref, offv_ref, tri_ref,
                       wg_ref, wu_ref, wd_ref, o_ref, xs_ref, ys_ref, gs_ref, dc_ref, *, epb, n_exp, slots):
    i = pl.program_id(0)
    j = pl.program_id(1)
    tm = t_ref.shape[0]

    @pl.when(j == 0)
    def _():
        t = t_ref[...]
        route = route_ref[...]
        lane = lax.broadcasted_iota(I32, (tm, LANES), 1)
        e1 = route[:, ROUTE_LANE:ROUTE_LANE + 1].astype(I32)
        e2 = route[:, ROUTE_LANE + 1:ROUTE_LANE + 2].astype(I32)
        pick1, pick2 = lane == e1, lane == e2
        picked = jnp.where(pick1 | pick2, 1.0, 0.0).astype(BF16)
        ahead = _dot(tri_ref[...], picked) + (offv_ref[0:1, :] - 1.0)
        s1 = jnp.sum(jnp.where(pick1, ahead, 0.0), axis=1, keepdims=True)
        s2 = jnp.sum(jnp.where(pick2, ahead, 0.0), axis=1, keepdims=True)
        slots_c = jnp.where(lane == 0, s1, jnp.where(lane == 1, s2, -1.0))
        dc_ref[...] = slots_c.astype(I32)
        slots_t = slots_c.T.astype(I32)
        gate_w = jnp.where(lane == 0, route[:, ROUTE_LANE + 2:ROUTE_LANE + 3],
                           jnp.where(lane == 1, route[:, ROUTE_LANE + 3:ROUTE_LANE + 4], 0.0)).T
        d1, d2 = slots_t[0:1, :], slots_t[1:2, :]
        w1, w2 = gate_w[0:1, :], gate_w[1:2, :]
        for c in range(slots // MOE_CHUNK):
            rows = slice(c * MOE_CHUNK, (c + 1) * MOE_CHUNK)
            slot = lax.broadcasted_iota(I32, (MOE_CHUNK, tm), 0) + c * MOE_CHUNK
            m1, m2 = slot == d1, slot == d2
            sel = jnp.where(m1 | m2, 1.0, 0.0).astype(BF16)
            xs_ref[rows, :] = _dot(sel, t).astype(BF16)
            g = jnp.sum(jnp.where(m1, w1, 0.0) + jnp.where(m2, w2, 0.0), axis=1, keepdims=True)
            gs_ref[rows, :] = jnp.broadcast_to(g, (MOE_CHUNK, LANES))
        xs_ref[slots:, :] = jnp.zeros((xs_ref.shape[0] - slots, xs_ref.shape[1]), BF16)
        gs_ref[slots:, :] = jnp.zeros((gs_ref.shape[0] - slots, LANES), F32)
        ys_ref[...] = jnp.zeros_like(ys_ref)

    def expert_chunk(k, start, end=None):
        rows = pl.ds(pl.multiple_of(start, MOE_ROW_ALIGN), MOE_CHUNK)
        x = xs_ref[rows, :]
        a = _dot(x, wg_ref[k])
        b = _dot(x, wu_ref[k])
        g = gs_ref[rows, :]
        hh = (a / (1.0 + jnp.exp(-a))) * b * jnp.concatenate([g] * (a.shape[1] // LANES), axis=1)
        y = _dot(hh.astype(BF16), wd_ref[k])
        if end is not None:
            slot = start + lax.broadcasted_iota(I32, y.shape, 0)
            y = jnp.where(slot < end, y, ys_ref[rows, :].astype(F32))
        ys_ref[rows, :] = y.astype(BF16)

    offs = [off_ref[i * n_exp + j * epb + k] for k in range(epb)]
    for k in range(epb):
        expert_chunk(k, offs[k])
    for k in range(epb):
        n_chunks = nch_ref[i * n_exp + j * epb + k]
        end = end_ref[i * n_exp + j * epb + k]

        def more(c, carry, k=k, end=end):
            expert_chunk(k, offs[k] + c * MOE_CHUNK, end)
            return carry

        lax.fori_loop(1, n_chunks, more, 0)

    @pl.when(j == pl.num_programs(1) - 1)
    def _():
        dc1, dc2 = dc_ref[:, 0:1], dc_ref[:, 1:2]
        acc = x1_ref[...]
        kw = 2 * LANES
        for kc in range(slots // kw):
            slot = lax.broadcasted_iota(I32, (tm, kw), 1) + kc * kw
            st = jnp.where((slot == dc1) | (slot == dc2), 1.0, 0.0).astype(BF16)
            acc = acc + _dot(st, ys_ref[kc * kw:(kc + 1) * kw, :])
        o_ref[...] = acc


def _moe_routed(t_bf, route, x1, wg, wu, wd, layer, tm, epb):
    n, d = x1.shape
    n_exp, _, dh = wg.shape[1:]
    n_tiles = n // tm
    slots = -(-(tm * TOP_K_INNER + n_exp * (MOE_ROW_ALIGN - 1)) // (2 * LANES)) * (2 * LANES)
    e12 = route[:, ROUTE_LANE:ROUTE_LANE + TOP_K_INNER].astype(I32).reshape(n_tiles, tm * TOP_K_INNER)
    cnt = jnp.sum(jax.nn.one_hot(e12, n_exp, dtype=I32), axis=1)
    cnt_al = -(-cnt // MOE_ROW_ALIGN) * MOE_ROW_ALIGN
    off = jnp.cumsum(cnt_al, axis=1) - cnt_al
    n_chunks = -(-cnt // MOE_CHUNK)
    off_v = jnp.pad(off.astype(F32)[:, None, :], ((0, 0), (0, 7), (0, LANES - n_exp)))
    tri = jnp.asarray(np.tril(np.ones((tm, tm), np.float32)), dtype=BF16)
    row = lambda i, j, *tables: (i, 0)
    tile3 = lambda i, j, *tables: (i, 0, 0)
    const = lambda i, j, *tables: (0, 0)
    wmap = lambda i, j, *tables: (layer, j, 0, 0)
    n_rows = slots + MOE_CHUNK
    grid_spec = pltpu.PrefetchScalarGridSpec(
        num_scalar_prefetch=3,
        grid=(n_tiles, n_exp // epb),
        in_specs=[pl.BlockSpec((tm, d), row), pl.BlockSpec((tm, d), row), pl.BlockSpec((tm, LANES), row),
                  pl.BlockSpec((None, 8, LANES), tile3), pl.BlockSpec((tm, tm), const),
                  pl.BlockSpec((None, epb, d, dh), wmap), pl.BlockSpec((None, epb, d, dh), wmap),
                  pl.BlockSpec((None, epb, dh, d), wmap)],
        out_specs=pl.BlockSpec((tm, d), row),
        scratch_shapes=[pltpu.VMEM((n_rows, d), BF16), pltpu.VMEM((n_rows, d), BF16),
                        pltpu.VMEM((n_rows, LANES), F32), pltpu.VMEM((tm, LANES), I32)],
    )
    return pl.pallas_call(
        functools.partial(_moe_routed_kernel, epb=epb, n_exp=n_exp, slots=slots),
        out_shape=jax.ShapeDtypeStruct((n, d), F32),
        grid_spec=grid_spec,
        compiler_params=_cparams("parallel", "arbitrary"),
        name="moe_routed",
    )(off.reshape(-1).astype(I32), (off + cnt_al).reshape(-1).astype(I32), n_chunks.reshape(-1).astype(I32),
      t_bf, x1, route, off_v, tri, wg, wu, wd)


def _block_diag_rows(w):
    g, a, b = w.shape
    eye = jnp.eye(g, dtype=w.dtype)
    return jnp.einsum('gab,gh->gahb', w, eye).reshape(g * a, g * b)


def _seg_cumsum_matrix(rows, seg):
    i = np.arange(rows)[:, None]
    j = np.arange(rows)[None, :]
    l = (j <= i) & (i // seg == j // seg)
    return jnp.asarray(np.concatenate([l, l], axis=1), dtype=BF16)


def kernel(x_prompt, x_sample, cache_k, cache_v, state_ssm_re, state_ssm_im, state_pool, page_table, norm_mix_g, w_in, q_norm_g, k_norm_g, sb_bias, ssm_lambda_re, ssm_lambda_im, ssm_log_dt, ssm_b_re, ssm_b_im, ssm_c_re, ssm_c_im, ssm_d, ssm_glu_w, ssm_glu_b, pool_w, pool_scale, w_out, norm_ffn_g, router_group_w, router_group_b, router_expert_w, router_expert_b, expert_w_gate, expert_w_up, expert_w_down):
    depth = w_in.shape[0]
    bsz, seq, d = x_prompt.shape
    db, dseq, _ = x_sample.shape
    n_phys, page, nh, hd = cache_k.shape[1:]
    assert hd == HEAD_DIM
    aw = nh * hd
    n_grp_ssm, n_state, grp_ch = ssm_b_re.shape[1:]
    sw = n_grp_ssm * grp_ch
    ns = n_grp_ssm * n_state
    pool_buf = state_pool.shape[2]
    assert pool_buf < POOL_HIST and max(POOL_WINDOWS) - 1 <= pool_buf
    n_exp_grp = router_group_w.shape[2]
    n_exp = router_expert_w.shape[2]
    assert n_exp + n_exp_grp <= ROUTE_LANE and ROUTE_LANE + 2 * TOP_K_INNER <= LANES
    past_len = page_table.shape[1] * page
    np_tok = bsz * seq
    ns_tok = db * dseq
    t_rows = 128

    w_in_bf = w_in.astype(BF16)
    wg_bf = expert_w_gate.astype(BF16)
    wu_bf = expert_w_up.astype(BF16)
    wd_bf = expert_w_down.astype(BF16)
    glu_w_bf = ssm_glu_w.astype(BF16)
    ones_bd = jnp.asarray(np.kron(np.eye(nh), np.ones((hd, hd))), dtype=BF16)
    cache_kt = cache_k.transpose(0, 1, 3, 4, 2).reshape(depth, n_phys, aw, page)
    cache_vt = cache_v.transpose(0, 1, 3, 4, 2).reshape(depth, n_phys, aw, page)
    head_eye = jnp.eye(nh, dtype=BF16)
    l2_prompt = _seg_cumsum_matrix(t_rows, t_rows)
    l2_sample = _seg_cumsum_matrix(ns_tok, dseq)
    seg_rows = POOL_HIST + dseq

    xp = x_prompt.reshape(np_tok, d)
    xs = x_sample.reshape(ns_tok, d)
    outs = {k: [] for k in ("kp", "vp", "hrp", "hip", "pbp", "ks", "vs", "hrs", "his", "pbs")}

    for l in range(depth):
        g_mix = norm_mix_g[l][None, :]
        qg_t = jnp.tile(q_norm_g[l], nh)[None, :]
        kg_t = jnp.tile(k_norm_g[l], nh)[None, :]
        bias_rows = jnp.broadcast_to(sb_bias[l][:, None], (nh, LANES))
        bias_rows_s = jnp.broadcast_to(jnp.repeat(sb_bias[l], dseq)[:, None], (nh * dseq, LANES))
        lr = ssm_lambda_re[l].reshape(1, ns)
        li = ssm_lambda_im[l].reshape(1, ns)
        ldt = jnp.broadcast_to(ssm_log_dt[l][:, None], (n_grp_ssm, n_state)).reshape(1, ns)
        bre_bd = _block_diag_rows(jnp.swapaxes(ssm_b_re[l], 1, 2))
        bim_bd = _block_diag_rows(jnp.swapaxes(ssm_b_im[l], 1, 2))
        cre_bd = _block_diag_rows(jnp.swapaxes(ssm_c_re[l], 1, 2))
        cim_bd = _block_diag_rows(jnp.swapaxes(ssm_c_im[l], 1, 2))
        prep = _ssm_prep(lr, li, ldt, bre_bd, bim_bd, cre_bd, cim_bd, t_rows)
        prep_s = tuple(prep[:3]) + tuple(jnp.tile(tb[:dseq], (db, 1)) for tb in prep[3:])
        dskip = ssm_d[l][None, :]
        glub = ssm_glu_b[l][None, :]
        pw_bd = _block_diag_rows(pool_w[l]).astype(BF16)
        pscale = pool_scale[l][None, :]
        g_ffn = norm_ffn_g[l][None, :]
        lane_pad = ((0, 0), (0, LANES - n_exp - n_exp_grp))
        rw = jnp.pad(jnp.concatenate([router_expert_w[l], router_group_w[l]], axis=1), lane_pad)
        rw_hi, rw_lo = _split_weights(rw)
        wo_hi, wo_lo = _split_weights(w_out[l])
        rb = jnp.pad(jnp.concatenate([router_expert_b[l], router_group_b[l]])[None, :], lane_pad)

        q_bf, k_pg, v_pg, k_bf, v_bf, us, up = _inproj(xp, g_mix, w_in_bf[l], qg_t, kg_t, ones_bd, tm=512,
                                                        kv_pages=True)
        attn = _attn_prompt(q_bf, k_bf, v_bf, bias_rows, bsz, seq, tq=1024)
        sso, poo, hlast = _mixers_prompt(us, up, prep, l2_prompt, dskip, glu_w_bf[l], glub, pw_bd, pscale,
                                         bsz, seq, t_rows)
        x1, t_bf, gates = _outproj(xp, attn, sso, poo, wo_hi, wo_lo, g_ffn, rw_hi, rw_lo, rb,
                                   n_exp, n_exp_grp, tm=512)
        xp = _moe_routed(t_bf, gates, x1, wg_bf, wu_bf, wd_bf, l, tm=1024, epb=4)
        outs["kp"].append(k_pg)
        outs["vp"].append(v_pg)
        outs["hrp"].append(hlast[:, 0, :ns].reshape(bsz, n_grp_ssm, n_state))
        outs["hip"].append(hlast[:, 0, ns:].reshape(bsz, n_grp_ssm, n_state))
        outs["pbp"].append(up.reshape(bsz, seq, sw)[:, seq - pool_buf:, :])

        q_bf, k32, v32, k_bf, v_bf, us, up = _inproj(xs, g_mix, w_in_bf[l], qg_t, kg_t, ones_bd, tm=ns_tok)
        qbd = jnp.einsum('bthd,hg->bhtgd', q_bf.reshape(db, dseq, nh, hd), head_eye).reshape(db, nh * dseq, aw)
        pad_rows = ((0, 0), (0, LANES - dseq), (0, 0))
        attn = _attn_sample(qbd, jnp.pad(k32.reshape(db, dseq, aw), pad_rows), jnp.pad(v32.reshape(db, dseq, aw), pad_rows),
                            cache_kt, cache_vt, page_table, bias_rows_s, l, n_pages_step=16, nq=dseq)
        ext = jnp.concatenate([jnp.zeros((db, POOL_HIST - pool_buf, sw), F32), state_pool[l],
                               up.reshape(db, dseq, sw)], axis=1)
        h0 = jnp.concatenate([state_ssm_re[l].reshape(db, ns), state_ssm_im[l].reshape(db, ns)], axis=1)
        h0rep = jnp.repeat(h0, dseq, axis=0)
        sso, poo_ext, hall = _mixers_sample(us, ext.reshape(db * seg_rows, sw), h0rep, prep_s, l2_sample, dskip,
                                            glu_w_bf[l], glub, pw_bd, pscale, seg_rows, past_len)
        poo = poo_ext.reshape(db, seg_rows, sw)[:, POOL_HIST:, :].reshape(ns_tok, sw)
        x1, t_bf, gates = _outproj(xs, attn.reshape(ns_tok, aw), sso, poo, wo_hi, wo_lo, g_ffn, rw_hi, rw_lo, rb,
                                   n_exp, n_exp_grp, tm=ns_tok)
        xs = _moe(t_bf, gates, x1, wg_bf, wu_bf, wd_bf, l, tm=ns_tok, epb=2)
        hl = hall.reshape(db, dseq, 2 * ns)[:, dseq - 1, :]
        outs["ks"].append(k32.reshape(db, dseq, nh, hd))
        outs["vs"].append(v32.reshape(db, dseq, nh, hd))
        outs["hrs"].append(hl[:, :ns].reshape(db, n_grp_ssm, n_state))
        outs["his"].append(hl[:, ns:].reshape(db, n_grp_ssm, n_state))
        outs["pbs"].append(ext[:, seg_rows - pool_buf:, :])

    st = {k: jnp.stack(v) for k, v in outs.items()}
    for name in ("kp", "vp"):
        st[name] = st[name].reshape(depth, np_tok // page, nh, hd, page).transpose(0, 1, 4, 2, 3)
    return (xp.reshape(bsz, seq, d), xs.reshape(db, dseq, d), st["kp"], st["vp"], st["hrp"], st["hip"], st["pbp"],
            st["ks"], st["vs"], st["hrs"], st["his"], st["pbs"])
```
